```python
import jax, jax.numpy as jnp
from jax import lax
import numpy as np

D_MODEL = 4096
BATCH = 2
SEQ = 8192
DEPTH = 1

MEM_LEN = 256
MIX_WIDTH = D_MODEL
POOL_WIDTH = MIX_WIDTH // 2
ATTN_WIDTH = MIX_WIDTH - POOL_WIDTH
POOL_WINDOWS = (2, 4, 8, 16)
N_POOL_GROUPS = len(POOL_WINDOWS)
POOL_GROUP_WIDTH = POOL_WIDTH // N_POOL_GROUPS
HEAD_DIM = 128
MOBA_HEADS = ATTN_WIDTH // HEAD_DIM
MOBA_BLOCK = 256
MOBA_TOPK = 3
MOBA_Q_CHUNK = 32
MIX_IN_WIDTH = POOL_WIDTH + 3 * ATTN_WIDTH
ROPE_THETA = 10000.0
XATTN_HEADS = 4
XATTN_HEAD_DIM = D_MODEL // XATTN_HEADS
FFN_HIDDEN = -((-8 * D_MODEL) // (3 * 256)) * 256
LN_EPS = 1e-5
DEEPNORM_ALPHA = (2 * DEPTH) ** 0.25
DEEPNORM_BETA = (8 * DEPTH) ** -0.25

kernel_name = 'hybrid_pool_moba_deepnorm_block'


def layer_norm(x, g, b):
    xf = x.astype(jnp.float32)
    mu = jnp.mean(xf, axis=-1, keepdims=True)
    var = jnp.mean(jnp.square(xf - mu), axis=-1, keepdims=True)
    return ((xf - mu) * lax.rsqrt(var + LN_EPS) * g + b).astype(x.dtype)


def rope(t, pos):
    half = t.shape[-1] // 2
    inv = ROPE_THETA ** (-jnp.arange(half, dtype=jnp.float32) / half)
    ang = pos.astype(jnp.float32)[:, None] * inv[None, :]
    cos, sin = jnp.cos(ang), jnp.sin(ang)
    t1 = t[..., :half].astype(jnp.float32)
    t2 = t[..., half:].astype(jnp.float32)
    return jnp.concatenate([t1 * cos - t2 * sin, t2 * cos + t1 * sin], axis=-1).astype(t.dtype)


def pool_mixer(u, w_pool, scale):
    B, S, _ = u.shape
    u = u.reshape(B, S, N_POOL_GROUPS, POOL_GROUP_WIDTH)
    uf = u.astype(jnp.float32)
    cs = jnp.pad(jnp.cumsum(uf, axis=1), ((0, 0), (1, 0), (0, 0), (0, 0)))
    win = jnp.array(POOL_WINDOWS, dtype=jnp.int32)
    t = jnp.arange(S, dtype=jnp.int32)
    lo = jnp.maximum(t[:, None] + 1 - win[None, :], 0)
    cnt = jnp.minimum(t[:, None] + 1, win[None, :]).astype(jnp.float32)
    grp = jnp.arange(N_POOL_GROUPS)[None, :]
    mean = (cs[:, 1:] - cs[:, lo, grp]) / cnt[None, :, :, None]
    mixed = (mean - uf).astype(u.dtype)
    y = jnp.einsum('bsgc,gcd->bsgd', mixed, w_pool) * scale.reshape(N_POOL_GROUPS, POOL_GROUP_WIDTH)
    return y.reshape(B, S, POOL_WIDTH)


def _gather_blocks(blocks, idx):
    return jax.vmap(jax.vmap(lambda bl, i: bl[i]))(blocks, idx)


def moba_attention(q, k, v):
    B, H, S, hd = q.shape
    nb = -(-S // MOBA_BLOCK)
    sp = nb * MOBA_BLOCK
    pad = ((0, 0), (0, 0), (0, sp - S), (0, 0))
    q, k, v = jnp.pad(q, pad), jnp.pad(k, pad), jnp.pad(v, pad)
    kb = k.reshape(B, H, nb, MOBA_BLOCK, hd)
    vb = v.reshape(B, H, nb, MOBA_BLOCK, hd)
    kmean = jnp.mean(kb.astype(jnp.float32), axis=3)
    gate = jnp.einsum('bhtd,bhnd->bhtn', q.astype(jnp.float32), kmean)
    qblk_all = jnp.arange(sp) // MOBA_BLOCK
    gate = jnp.where(jnp.arange(nb)[None, :] < qblk_all[:, None], gate, -jnp.inf)
    k_sel = min(MOBA_TOPK, nb)
    _, sel_idx = lax.top_k(gate, k_sel)
    scale = HEAD_DIM ** -0.5
    key_off = jnp.arange(MOBA_BLOCK)
    C = MOBA_Q_CHUNK

    def step(s0):
        qc = lax.dynamic_slice_in_dim(q, s0, C, axis=2)
        ic = lax.dynamic_slice_in_dim(sel_idx, s0, C, axis=2)
        blk = s0 // MOBA_BLOCK
        kown = lax.dynamic_index_in_dim(kb, blk, axis=2, keepdims=False)
        vown = lax.dynamic_index_in_dim(vb, blk, axis=2, keepdims=False)
        ksel = _gather_blocks(kb, ic)
        vsel = _gather_blocks(vb, ic)
        qpos = s0 + jnp.arange(C)
        sel_ok = jnp.arange(k_sel)[None, :] < (qpos // MOBA_BLOCK)[:, None]
        own_ok = (blk * MOBA_BLOCK + key_off)[None, :] <= qpos[:, None]
        s_sel = jnp.einsum('bhcd,bhckpd->bhckp', qc, ksel).astype(jnp.float32) * scale
        s_sel = jnp.where(sel_ok[None, None, :, :, None], s_sel, -jnp.inf)
        s_sel = s_sel.reshape(B, H, C, k_sel * MOBA_BLOCK)
        s_own = jnp.einsum('bhcd,bhpd->bhcp', qc, kown).astype(jnp.float32) * scale
        s_own = jnp.where(own_ok[None, None], s_own, -jnp.inf)
        p = jax.nn.softmax(jnp.concatenate([s_sel, s_own], axis=-1), axis=-1)
        p_sel = p[..., :k_sel * MOBA_BLOCK].reshape(B, H, C, k_sel, MOBA_BLOCK).astype(v.dtype)
        p_own = p[..., k_sel * MOBA_BLOCK:].astype(v.dtype)
        return (jnp.einsum('bhckp,bhckpd->bhcd', p_sel, vsel)
                + jnp.einsum('bhcp,bhpd->bhcd', p_own, vown))

    out = lax.map(step, jnp.arange(sp // C, dtype=jnp.int32) * C)
    out = out.transpose(1, 2, 0, 3, 4).reshape(B, H, sp, hd)
    return out[:, :, :S]


def hybrid_mixer(x, w_in, w_pool, pool_scale, w_out, pos):
    B, S, _ = x.shape
    h = x @ w_in
    hp = h[..., :POOL_WIDTH]
    hq = h[..., POOL_WIDTH:POOL_WIDTH + ATTN_WIDTH]
    hk = h[..., POOL_WIDTH + ATTN_WIDTH:POOL_WIDTH + 2 * ATTN_WIDTH]
    hv = h[..., POOL_WIDTH + 2 * ATTN_WIDTH:]
    pool_out = pool_mixer(hp, w_pool, pool_scale)

    def to_heads(t):
        return t.reshape(B, S, MOBA_HEADS, HEAD_DIM).transpose(0, 2, 1, 3)

    q = rope(to_heads(hq), pos)
    k = rope(to_heads(hk), pos)
    a = moba_attention(q, k, to_heads(hv))
    a = a.transpose(0, 2, 1, 3).reshape(B, S, ATTN_WIDTH)
    return jnp.concatenate([pool_out, a], axis=-1) @ w_out


def memory_cross_attention(x, mem, wq, wkv, wo):
    B, S, D = x.shape
    M = mem.shape[1]
    q = (x @ wq).reshape(B, S, XATTN_HEADS, XATTN_HEAD_DIM)
    kv = mem @ wkv
    k = kv[..., :D].reshape(B, M, XATTN_HEADS, XATTN_HEAD_DIM)
    v = kv[..., D:].reshape(B, M, XATTN_HEADS, XATTN_HEAD_DIM)
    s = jnp.einsum('bshd,bmhd->bhsm', q, k).astype(jnp.float32) * XATTN_HEAD_DIM ** -0.5
    p = jax.nn.softmax(s, axis=-1).astype(x.dtype)
    o = jnp.einsum('bhsm,bmhd->bshd', p, v).reshape(B, S, D)
    return o @ wo


def swiglu_ffn(x, w_gate, w_up, w_down):
    return (jax.nn.silu(x @ w_gate) * (x @ w_up)) @ w_down


def _normal(key, shape, std):
    return std * jax.random.normal(key, shape, dtype=jnp.float32)


def setup_inputs(seed: int = 0) -> dict:
    key = jax.random.key(seed)
    ks = jax.random.split(key, 18)
    L, D = DEPTH, D_MODEL
    return {
        'x': _normal(ks[0], (BATCH, SEQ, D), 1.0),
        'mem': _normal(ks[1], (BATCH, MEM_LEN, D), 1.0),
        'w_mix_in': _normal(ks[2], (L, D, MIX_IN_WIDTH), D ** -0.5),
        'w_pool': _normal(ks[3], (L, N_POOL_GROUPS, POOL_GROUP_WIDTH, POOL_GROUP_WIDTH), POOL_GROUP_WIDTH ** -0.5),
        'pool_scale': 1.0 + _normal(ks[4], (L, POOL_WIDTH), 0.1),
        'w_mix_out': _normal(ks[5], (L, MIX_WIDTH, D), DEEPNORM_BETA * MIX_WIDTH ** -0.5),
        'ln1_g': 1.0 + _normal(ks[6], (L, D), 0.02),
        'ln1_b': _normal(ks[7], (L, D), 0.02),
        'w_xq': _normal(ks[8], (L, D, D), D ** -0.5),
        'w_xkv': _normal(ks[9], (L, D, 2 * D), D ** -0.5),
        'w_xo': _normal(ks[10], (L, D, D), DEEPNORM_BETA * D ** -0.5),
        'ln2_g': 1.0 + _normal(ks[11], (L, D), 0.02),
        'ln2_b': _normal(ks[12], (L, D), 0.02),
        'w_gate': _normal(ks[13], (L, D, FFN_HIDDEN), D ** -0.5),
        'w_up': _normal(ks[14], (L, D, FFN_HIDDEN), D ** -0.5),
        'w_down': _normal(ks[15], (L, FFN_HIDDEN, D), DEEPNORM_BETA * FFN_HIDDEN ** -0.5),
        'ln3_g': 1.0 + _normal(ks[16], (L, D), 0.02),
        'ln3_b': _normal(ks[17], (L, D), 0.02),
    }


def reference(x, mem, w_mix_in, w_pool, pool_scale, w_mix_out, ln1_g, ln1_b,
              w_xq, w_xkv, w_xo, ln2_g, ln2_b, w_gate, w_up, w_down, ln3_g, ln3_b):
    pos = jnp.arange(x.shape[1], dtype=jnp.int32)
    h = x
    for l in range(DEPTH):
        mix = hybrid_mixer(h, w_mix_in[l], w_pool[l], pool_scale[l], w_mix_out[l], pos)
        h = layer_norm(DEEPNORM_ALPHA * h + mix, ln1_g[l], ln1_b[l])
        xa = memory_cross_attention(h, mem, w_xq[l], w_xkv[l], w_xo[l])
        h = layer_norm(DEEPNORM_ALPHA * h + xa, ln2_g[l], ln2_b[l])
        ff = swiglu_ffn(h, w_gate[l], w_up[l], w_down[l])
        h = layer_norm(DEEPNORM_ALPHA * h + ff, ln3_g[l], ln3_b[l])
    return h
```

```python
import functools

import jax
import jax.numpy as jnp
from jax import lax
from jax.experimental import pallas as pl
from jax.experimental.pallas import tpu as pltpu

POOL_WINDOWS = (2, 4, 8, 16)
HEAD_DIM = 128
MOBA_BLOCK = 256
MOBA_TOPK = 3
ROPE_THETA = 10000.0
XATTN_HEADS = 4
LN_EPS = 1e-5
DEPTH = 1
DEEPNORM_ALPHA = (2 * DEPTH) ** 0.25

V7X_LANES = 128
V7X_VMEM_LIMIT_BYTES = 56 * 1024 * 1024
MASK_VALUE = -1e30
POOL_HALO = 16
FFN_TILE = 512

_F32 = jnp.float32
_BF16 = jnp.bfloat16
_NT = (((1,), (1,)), ((), ()))


def _params(*semantics):
    return pltpu.CompilerParams(dimension_semantics=semantics, vmem_limit_bytes=V7X_VMEM_LIMIT_BYTES)


def _dot(a, b):
    return jnp.dot(a, b, preferred_element_type=_F32)


def _dot_nt(a, b):
    return lax.dot_general(a, b, _NT, preferred_element_type=_F32)


def _mm_kernel(x_ref, w_ref, o_ref):
    o_ref[...] = _dot(x_ref[...], w_ref[...]).astype(o_ref.dtype)


def _matmul(x, w, *, tm, tn, out_dtype, name):
    m, k = x.shape
    n = w.shape[1]
    tm, tn = min(tm, m), min(tn, n)
    return pl.pallas_call(
        _mm_kernel,
        grid=(m // tm, n // tn),
        in_specs=[pl.BlockSpec((tm, k), lambda i, j: (i, 0)), pl.BlockSpec((k, tn), lambda i, j: (0, j))],
        out_specs=pl.BlockSpec((tm, tn), lambda i, j: (i, j)),
        out_shape=jax.ShapeDtypeStruct((m, n), out_dtype),
        compiler_params=_params("parallel", "parallel"),
        name=name,
    )(x, w)


def _mm_rope_kernel(x_ref, w_ref, cos_ref, sin_ref, o_ref, *, q_tiles, q_scale):
    acc = _dot(x_ref[...], w_ref[...])
    scale = jnp.where(pl.program_id(1) < q_tiles, q_scale, 1.0).astype(_F32)
    cos = cos_ref[...] * scale
    sin = sin_ref[...] * scale
    for h in range(acc.shape[1] // HEAD_DIM):
        cols = slice(h * HEAD_DIM, (h + 1) * HEAD_DIM)
        t = acc[:, cols]
        o_ref[:, cols] = (t * cos + pltpu.roll(t, HEAD_DIM // 2, 1) * sin).astype(o_ref.dtype)


def _matmul_rope(x, w, cos, sin, *, seq, q_width, q_scale, tm, tn, name):
    m, k = x.shape
    n = w.shape[1]
    tm, tn = min(tm, seq), min(tn, n)
    seq_tiles = seq // tm
    kern = functools.partial(_mm_rope_kernel, q_tiles=q_width // tn, q_scale=q_scale)
    return pl.pallas_call(
        kern,
        grid=(m // tm, n // tn),
        in_specs=[
            pl.BlockSpec((tm, k), lambda i, j: (i, 0)),
            pl.BlockSpec((k, tn), lambda i, j: (0, j)),
            pl.BlockSpec((tm, HEAD_DIM), lambda i, j: (i % seq_tiles, 0)),
            pl.BlockSpec((tm, HEAD_DIM), lambda i, j: (i % seq_tiles, 0)),
        ],
        out_specs=pl.BlockSpec((tm, tn), lambda i, j: (i, j)),
        out_shape=jax.ShapeDtypeStruct((m, n), _BF16),
        compiler_params=_params("parallel", "parallel"),
        name=name,
    )(x, w, cos, sin)


def _mm_vt_kernel(wt_ref, x_ref, o_ref):
    acc = _dot_nt(wt_ref[...], x_ref[...]).astype(o_ref.dtype)
    for h in range(o_ref.shape[0]):
        for kb in range(o_ref.shape[1]):
            o_ref[h, kb] = acc[h * HEAD_DIM:(h + 1) * HEAD_DIM, kb * MOBA_BLOCK:(kb + 1) * MOBA_BLOCK]


def _matmul_vt(wt, x, *, tm, tn, name):
    n, k = wt.shape
    m = x.shape[0]
    tm, tn = min(tm, m), min(tn, n)
    nh, nkb = tn // HEAD_DIM, tm // MOBA_BLOCK
    return pl.pallas_call(
        _mm_vt_kernel,
        grid=(m // tm, n // tn),
        in_specs=[pl.BlockSpec((tn, k), lambda i, j: (j, 0)), pl.BlockSpec((tm, k), lambda i, j: (i, 0))],
        out_specs=pl.BlockSpec((nh, nkb, HEAD_DIM, MOBA_BLOCK), lambda i, j: (j, i, 0, 0)),
        out_shape=jax.ShapeDtypeStruct((n // HEAD_DIM, m // MOBA_BLOCK, HEAD_DIM, MOBA_BLOCK), _BF16),
        compiler_params=_params("parallel", "parallel"),
        name=name,
    )(wt, x)


def _pool_kernel(halo_ref, u_ref, w_ref, scale_ref, o_ref, ext_ref, *, seq_tiles):
    tm = u_ref.shape[0]
    gw = w_ref.shape[1]
    first = pl.program_id(0) % seq_tiles == 0
    halo = halo_ref[...].astype(_F32)
    ext_ref[0:POOL_HALO, :] = jnp.where(first, 0.0, halo)
    ext_ref[POOL_HALO:, :] = u_ref[...].astype(_F32)
    pos = (pl.program_id(0) % seq_tiles) * tm + lax.broadcasted_iota(jnp.int32, (tm, gw), 0)
    for g, win in enumerate(POOL_WINDOWS):
        cols = slice(g * gw, (g + 1) * gw)
        u = ext_ref[POOL_HALO:, cols]
        s = u
        for d in range(1, win):
            s = s + ext_ref[POOL_HALO - d:POOL_HALO - d + tm, cols]
        cnt = jnp.minimum(pos + 1, win).astype(_F32)
        mixed = (s / cnt - u).astype(_BF16)
        y = _dot(mixed, w_ref[g]) * scale_ref[:, cols]
        o_ref[:, cols] = y.astype(o_ref.dtype)


def _pool_mixer(hp, w_pool, scale, *, seq, tm, name):
    m, width = hp.shape
    tm = min(tm, seq)
    seq_tiles = seq // tm
    halo_per_tile = tm // POOL_HALO
    return pl.pallas_call(
        functools.partial(_pool_kernel, seq_tiles=seq_tiles),
        grid=(m // tm,),
        in_specs=[
            pl.BlockSpec((POOL_HALO, width), lambda i: (jnp.maximum(i * halo_per_tile - 1, 0), 0)),
            pl.BlockSpec((tm, width), lambda i: (i, 0)),
            pl.BlockSpec(w_pool.shape, lambda i: (0, 0, 0)),
            pl.BlockSpec((1, width), lambda i: (0, 0)),
        ],
        out_specs=pl.BlockSpec((tm, width), lambda i: (i, 0)),
        out_shape=jax.ShapeDtypeStruct((m, width), _BF16),
        scratch_shapes=[pltpu.VMEM((tm + POOL_HALO, width), _F32)],
        compiler_params=_params("parallel"),
        name=name,
    )(hp, hp, w_pool, scale)


def _moba_kernel(q_ref, k_ref, vt_ref, o_ref, kmean_ref, bias_ref, m_ref, l_ref, acc_ref, *, heads):
    n = pl.program_id(2)
    nb = kmean_ref.shape[1]
    bq = q_ref.shape[0]

    @pl.when(n == 0)
    def _():
        def block_mean(j, carry):
            rows = pl.ds(pl.multiple_of(j * MOBA_BLOCK, MOBA_BLOCK), MOBA_BLOCK)
            for h in range(heads):
                kb = k_ref[rows, h * HEAD_DIM:(h + 1) * HEAD_DIM].astype(_F32)
                kmean_ref[h, pl.ds(j, 1), :] = jnp.mean(kb, axis=0, keepdims=True)
            return carry
        lax.fori_loop(0, nb, block_mean, 0)

    own_rows = pl.ds(pl.multiple_of(n * MOBA_BLOCK, MOBA_BLOCK), MOBA_BLOCK)
    blk = lax.broadcasted_iota(jnp.int32, (nb, bq), 0).astype(_F32)
    n_f = n.astype(_F32)
    kidx = lax.broadcasted_iota(jnp.int32, (MOBA_BLOCK, bq), 0)
    qidx = lax.broadcasted_iota(jnp.int32, (MOBA_BLOCK, bq), 1)
    for h in range(heads):
        cols = slice(h * HEAD_DIM, (h + 1) * HEAD_DIM)
        q = q_ref[:, cols]
        km = kmean_ref[h]
        km_hi = km.astype(_BF16)
        km_lo = (km - km_hi.astype(_F32)).astype(_BF16)
        gate = _dot_nt(km_hi, q) + _dot_nt(km_lo, q)
        g = jnp.where(blk < n_f, gate, -jnp.inf)
        bias = jnp.full((nb, bq), MASK_VALUE, _F32)
        for _ in range(MOBA_TOPK):
            top = jnp.max(g, axis=0, keepdims=True)
            is_top = (g == top) & (g > -jnp.inf)
            first = jnp.min(jnp.where(is_top, blk, float(nb)), axis=0, keepdims=True)
            pick = blk == first
            bias = jnp.where(pick, 0.0, bias)
            g = jnp.where(pick, -jnp.inf, g)
        bias_ref[h] = bias
        s = _dot_nt(k_ref[own_rows, cols], q)
        s = jnp.where(kidx <= qidx, s, MASK_VALUE)
        m = jnp.max(s, axis=0, keepdims=True)
        p = jnp.exp(s - m)
        m_ref[h] = m
        l_ref[h] = jnp.sum(p, axis=0, keepdims=True)
        acc_ref[h] = _dot(vt_ref[h, n], p.astype(_BF16))

    def past_block(j, carry):
        rows = pl.ds(pl.multiple_of(j * MOBA_BLOCK, MOBA_BLOCK), MOBA_BLOCK)
        for h in range(heads):
            cols = slice(h * HEAD_DIM, (h + 1) * HEAD_DIM)
            s = _dot_nt(k_ref[rows, cols], q_ref[:, cols]) + bias_ref[h, pl.ds(j, 1), :]
            m_old = m_ref[h]
            m_new = jnp.maximum(m_old, jnp.max(s, axis=0, keepdims=True))
            alpha = jnp.exp(m_old - m_new)
            p = jnp.exp(s - m_new)
            m_ref[h] = m_new
            l_ref[h] = alpha * l_ref[h] + jnp.sum(p, axis=0, keepdims=True)
            acc_ref[h] = alpha * acc_ref[h] + _dot(vt_ref[h, j], p.astype(_BF16))
        return carry
    lax.fori_loop(0, n, past_block, 0)

    for h in range(heads):
        out_t = acc_ref[h] / l_ref[h]
        o_ref[:, h * HEAD_DIM:(h + 1) * HEAD_DIM] = out_t.T.astype(o_ref.dtype)


def _moba_attention(qk, vt, *, batch, seq, attn_width, heads_per_step, name):
    t = qk.shape[0]
    nb = seq // MOBA_BLOCK
    gw = heads_per_step * HEAD_DIM
    n_groups = attn_width // gw
    return pl.pallas_call(
        functools.partial(_moba_kernel, heads=heads_per_step),
        grid=(batch, n_groups, nb),
        in_specs=[
            pl.BlockSpec((MOBA_BLOCK, gw), lambda b, g, n: (b * nb + n, g)),
            pl.BlockSpec((seq, gw), lambda b, g, n: (b, n_groups + g)),
            pl.BlockSpec((heads_per_step, nb, HEAD_DIM, MOBA_BLOCK), lambda b, g, n: (g, b, 0, 0)),
        ],
        out_specs=pl.BlockSpec((MOBA_BLOCK, gw), lambda b, g, n: (b * nb + n, g)),
        out_shape=jax.ShapeDtypeStruct((t, attn_width), _BF16),
        scratch_shapes=[
            pltpu.VMEM((heads_per_step, nb, HEAD_DIM), _F32),
            pltpu.VMEM((heads_per_step, nb, MOBA_BLOCK), _F32),
            pltpu.VMEM((heads_per_step, 1, MOBA_BLOCK), _F32),
            pltpu.VMEM((heads_per_step, 1, MOBA_BLOCK), _F32),
            pltpu.VMEM((heads_per_step, HEAD_DIM, MOBA_BLOCK), _F32),
        ],
        compiler_params=_params("parallel", "parallel", "arbitrary"),
        name=name,
    )(qk, qk, vt)


def _mm_ln_kernel(a_ref, w_ref, r_ref, g_ref, b_ref, o_ref, *maybe_ob_ref, k_tiles, row_chunk, col_chunk):
    kk = pl.program_id(1)

    @pl.when(kk == 0)
    def _():
        o_ref[...] = DEEPNORM_ALPHA * r_ref[...]

    a = a_ref[...]
    for c in range(o_ref.shape[1] // col_chunk):
        cols = slice(c * col_chunk, (c + 1) * col_chunk)
        o_ref[:, cols] += _dot(a, w_ref[:, cols])

    @pl.when(kk == k_tiles - 1)
    def _():
        gain = g_ref[...]
        shift = b_ref[...]
        for r in range(o_ref.shape[0] // row_chunk):
            rows = slice(r * row_chunk, (r + 1) * row_chunk)
            y = o_ref[rows, :]
            mu = jnp.mean(y, axis=-1, keepdims=True)
            yc = y - mu
            var = jnp.mean(yc * yc, axis=-1, keepdims=True)
            h = yc * lax.rsqrt(var + LN_EPS) * gain + shift
            o_ref[rows, :] = h
            for ob_ref in maybe_ob_ref:
                ob_ref[rows, :] = h.astype(ob_ref.dtype)


def _matmul_ln(a, w, resid, gain, shift, *, tm, tk, with_bf16, name):
    m, k = a.shape
    n = w.shape[1]
    tm, tk = min(tm, m), min(tk, k)
    k_tiles = k // tk
    out_shape = [jax.ShapeDtypeStruct((m, n), _F32)]
    out_specs = [pl.BlockSpec((tm, n), lambda i, kk: (i, 0))]
    if with_bf16:
        out_shape.append(jax.ShapeDtypeStruct((m, n), _BF16))
        out_specs.append(pl.BlockSpec((tm, n), lambda i, kk: (i, 0)))
    outs = pl.pallas_call(
        functools.partial(_mm_ln_kernel, k_tiles=k_tiles, row_chunk=min(64, tm), col_chunk=min(1024, n)),
        grid=(m // tm, k_tiles),
        in_specs=[
            pl.BlockSpec((tm, tk), lambda i, kk: (i, kk)),
            pl.BlockSpec((tk, n), lambda i, kk: (kk, 0)),
            pl.BlockSpec((tm, n), lambda i, kk: (i, 0), pipeline_mode=pl.Buffered(1)),
            pl.BlockSpec((1, n), lambda i, kk: (0, 0)),
            pl.BlockSpec((1, n), lambda i, kk: (0, 0)),
        ],
        out_specs=out_specs,
        out_shape=out_shape,
        compiler_params=_params("parallel", "arbitrary"),
        name=name,
    )(a, w, resid, gain, shift)
    return outs if with_bf16 else outs[0]


def _xattn_kernel(h_ref, wq_ref, k_ref, v_ref, o_ref, *, scale):
    q = _dot(h_ref[...], wq_ref[...]).astype(_BF16)
    s = _dot_nt(q, k_ref[...]) * scale
    m = jnp.max(s, axis=-1, keepdims=True)
    p = jnp.exp(s - m)
    l = jnp.sum(p, axis=-1, keepdims=True)
    o = _dot(p.astype(_BF16), v_ref[...]) / l
    o_ref[...] = o.astype(o_ref.dtype)


def _cross_attention(h, wq, kv, *, seq, mem_len, tm, name):
    t, d = h.shape
    hd = d // XATTN_HEADS
    tm = min(tm, seq)
    seq_tiles = seq // tm
    return pl.pallas_call(
        functools.partial(_xattn_kernel, scale=hd ** -0.5),
        grid=(t // tm, XATTN_HEADS),
        in_specs=[
            pl.BlockSpec((tm, d), lambda i, hh: (i, 0)),
            pl.BlockSpec((d, hd), lambda i, hh: (0, hh)),
            pl.BlockSpec((mem_len, hd), lambda i, hh: (i // seq_tiles, hh)),
            pl.BlockSpec((mem_len, hd), lambda i, hh: (i // seq_tiles, XATTN_HEADS + hh)),
        ],
        out_specs=pl.BlockSpec((tm, hd), lambda i, hh: (i, hh)),
        out_shape=jax.ShapeDtypeStruct((t, d), _BF16),
        compiler_params=_params("parallel", "parallel"),
        name=name,
    )(h, wq, kv, kv)


def _gate_up_kernel(x_ref, wg_ref, wu_ref, o_ref):
    x = x_ref[...]
    g = _dot(x, wg_ref[...])
    u = _dot(x, wu_ref[...])
    o_ref[...] = (g * jax.nn.sigmoid(g) * u).astype(o_ref.dtype)


def _gate_up(x, wg, wu, *, tm, tn, name):
    m, k = x.shape
    n = wg.shape[1]
    tm, tn = min(tm, m), min(tn, n)
    return pl.pallas_call(
        _gate_up_kernel,
        grid=(m // tm, n // tn),
        in_specs=[
            pl.BlockSpec((tm, k), lambda i, j: (i, 0)),
            pl.BlockSpec((k, tn), lambda i, j: (0, j)),
            pl.BlockSpec((k, tn), lambda i, j: (0, j)),
        ],
        out_specs=pl.BlockSpec((tm, tn), lambda i, j: (i, j)),
        out_shape=jax.ShapeDtypeStruct((m, n), _BF16),
        compiler_params=_params("parallel", "parallel"),
        name=name,
    )(x, wg, wu)


def _rope_tables(seq):
    half = HEAD_DIM // 2
    inv = ROPE_THETA ** (-jnp.arange(half, dtype=_F32) / half)
    ang = jnp.arange(seq, dtype=jnp.int32).astype(_F32)[:, None] * inv[None, :]
    cos, sin = jnp.cos(ang), jnp.sin(ang)
    return jnp.concatenate([cos, cos], axis=-1), jnp.concatenate([-sin, sin], axis=-1)


def _layer(h, hb, mem_b, w_in, w_pool, pool_scale, w_out, ln1_g, ln1_b, w_xq, w_xkv, w_xo, ln2_g, ln2_b,
           w_gate, w_up, w_down, ln3_g, ln3_b, *, batch, seq):
    d = h.shape[1]
    pool_width = w_pool.shape[0] * w_pool.shape[1]
    attn_width = (w_in.shape[1] - pool_width) // 3
    mem_len = mem_b.shape[0] // batch
    hidden = w_gate.shape[1]
    hidden_pad = -(-hidden // FFN_TILE) * FFN_TILE

    w_p = w_in[:, :pool_width].astype(_BF16)
    w_qk = w_in[:, pool_width:pool_width + 2 * attn_width].astype(_BF16)
    w_vt = w_in[:, pool_width + 2 * attn_width:].T.astype(_BF16)
    pad = ((0, 0), (0, hidden_pad - hidden))
    w_gate_b = jnp.pad(w_gate.astype(_BF16), pad)
    w_up_b = jnp.pad(w_up.astype(_BF16), pad)
    w_down_b = jnp.pad(w_down.astype(_BF16), (pad[1], pad[0]))
    cos, sin = _rope_tables(seq)
    row = lambda v: v.reshape(1, -1)

    hp = _matmul(hb, w_p, tm=1024, tn=1024, out_dtype=_BF16, name="in_proj_pool")
    qk = _matmul_rope(hb, w_qk, cos, sin, seq=seq, q_width=attn_width, q_scale=HEAD_DIM ** -0.5,
                      tm=1024, tn=1024, name="in_proj_qk_rope")
    vt = _matmul_vt(w_vt, hb, tm=1024, tn=1024, name="in_proj_vt")
    pool_out = _pool_mixer(hp, w_pool.astype(_BF16), row(pool_scale), seq=seq, tm=512, name="pool_mixer")
    attn = _moba_attention(qk, vt, batch=batch, seq=seq, attn_width=attn_width, heads_per_step=4,
                           name="moba_attention")
    mix_in = jnp.concatenate([pool_out, attn], axis=-1)
    h1, h1b = _matmul_ln(mix_in, w_out.astype(_BF16), h, row(ln1_g), row(ln1_b), tm=512, tk=512,
                         with_bf16=True, name="mix_out_ln1")

    kv = _matmul(mem_b, w_xkv.astype(_BF16), tm=512, tn=1024, out_dtype=_BF16, name="xattn_kv")
    xo = _cross_attention(h1b, w_xq.astype(_BF16), kv, seq=seq, mem_len=mem_len, tm=1024, name="xattn_q_attend")
    h2, h2b = _matmul_ln(xo, w_xo.astype(_BF16), h1, row(ln2_g), row(ln2_b), tm=512, tk=512,
                         with_bf16=True, name="xattn_out_ln2")

    act = _gate_up(h2b, w_gate_b, w_up_b, tm=1024, tn=FFN_TILE, name="ffn_gate_up")
    h3 = _matmul_ln(act, w_down_b, h2, row(ln3_g), row(ln3_b), tm=512, tk=FFN_TILE,
                    with_bf16=False, name="ffn_down_ln3")
    return h3


def kernel(x, mem, w_mix_in, w_pool, pool_scale, w_mix_out, ln1_g, ln1_b, w_xq, w_xkv, w_xo, ln2_g, ln2_b,
           w_gate, w_up, w_down, ln3_g, ln3_b):
    batch, seq, d = x.shape
    assert w_mix_in.shape[0] == DEPTH
    h = x.reshape(batch * seq, d)
    mem_b = mem.reshape(-1, d).astype(_BF16)
    for l in range(DEPTH):
        h = _layer(h, h.astype(_BF16), mem_b, w_mix_in[l], w_pool[l], pool_scale[l], w_mix_out[l],
                   ln1_g[l], ln1_b[l], w_xq[l], w_xkv[l], w_xo[l], ln2_g[l], ln2_b[l],
                   w_gate[l], w_up[l], w_down[l], ln3_g[l], ln3_b[l], batch=batch, seq=seq)
    return h.reshape(batch, seq, d)
```

```python
import functools

import jax
import jax.numpy as jnp
from jax import lax
from jax.experimental import pallas as pl
from jax.experimental.pallas import tpu as pltpu

POOL_WINDOWS = (2, 4, 8, 16)
HEAD_DIM = 128
MOBA_BLOCK = 256
MOBA_TOPK = 3
ROPE_THETA = 10000.0
XATTN_HEADS = 4
LN_EPS = 1e-5
DEPTH = 1
DEEPNORM_ALPHA = (2 * DEPTH) ** 0.25

V7X_LANES = 128
V7X_VMEM_LIMIT_BYTES = 56 * 1024 * 1024
MASK_VALUE = -1e30
BIAS_PAD_ROWS = 8
POOL_HALO = 16
FFN_TILE = 512

_F32 = jnp.float32
_BF16 = jnp.bfloat16
_NT = (((1,), (1,)), ((), ()))


def _params(*semantics):
    return pltpu.CompilerParams(dimension_semantics=semantics, vmem_limit_bytes=V7X_VMEM_LIMIT_BYTES)


def _dot(a, b):
    return jnp.dot(a, b, preferred_element_type=_F32)


def _dot_nt(a, b):
    return lax.dot_general(a, b, _NT, preferred_element_type=_F32)


def _mm_kernel(x_ref, w_ref, o_ref):
    o_ref[...] = _dot(x_ref[...], w_ref[...]).astype(o_ref.dtype)


def _matmul(x, w, *, tm, tn, out_dtype, name):
    m, k = x.shape
    n = w.shape[1]
    tm, tn = min(tm, m), min(tn, n)
    return pl.pallas_call(
        _mm_kernel,
        grid=(m // tm, n // tn),
        in_specs=[pl.BlockSpec((tm, k), lambda i, j: (i, 0)), pl.BlockSpec((k, tn), lambda i, j: (0, j))],
        out_specs=pl.BlockSpec((tm, tn), lambda i, j: (i, j)),
        out_shape=jax.ShapeDtypeStruct((m, n), out_dtype),
        compiler_params=_params("parallel", "parallel"),
        name=name,
    )(x, w)


def _mm_rope_kernel(x_ref, w_ref, cos_ref, sin_ref, o_ref, *, q_tiles, q_scale):
    acc = _dot(x_ref[...], w_ref[...])
    scale = jnp.where(pl.program_id(1) < q_tiles, q_scale, 1.0).astype(_F32)
    cos = cos_ref[...] * scale
    sin = sin_ref[...] * scale
    for h in range(acc.shape[1] // HEAD_DIM):
        cols = slice(h * HEAD_DIM, (h + 1) * HEAD_DIM)
        t = acc[:, cols]
        o_ref[:, cols] = (t * cos + pltpu.roll(t, HEAD_DIM // 2, 1) * sin).astype(o_ref.dtype)


def _matmul_rope(x, w, cos, sin, *, seq, q_width, q_scale, tm, tn, name):
    m, k = x.shape
    n = w.shape[1]
    tm, tn = min(tm, seq), min(tn, n)
    seq_tiles = seq // tm
    kern = functools.partial(_mm_rope_kernel, q_tiles=q_width // tn, q_scale=q_scale)
    return pl.pallas_call(
        kern,
        grid=(m // tm, n // tn),
        in_specs=[
            pl.BlockSpec((tm, k), lambda i, j: (i, 0)),
            pl.BlockSpec((k, tn), lambda i, j: (0, j)),
            pl.BlockSpec((tm, HEAD_DIM), lambda i, j: (i % seq_tiles, 0)),
            pl.BlockSpec((tm, HEAD_DIM), lambda i, j: (i % seq_tiles, 0)),
        ],
        out_specs=pl.BlockSpec((tm, tn), lambda i, j: (i, j)),
        out_shape=jax.ShapeDtypeStruct((m, n), _BF16),
        compiler_params=_params("parallel", "parallel"),
        name=name,
    )(x, w, cos, sin)


def _mm_vt_kernel(wt_ref, x_ref, o_ref):
    acc = _dot_nt(wt_ref[...], x_ref[...]).astype(o_ref.dtype)
    for h in range(o_ref.shape[0]):
        for kb in range(o_ref.shape[1]):
            o_ref[h, kb] = acc[h * HEAD_DIM:(h + 1) * HEAD_DIM, kb * MOBA_BLOCK:(kb + 1) * MOBA_BLOCK]


def _matmul_vt(wt, x, *, tm, tn, name):
    n, k = wt.shape
    m = x.shape[0]
    tm, tn = min(tm, m), min(tn, n)
    nh, nkb = tn // HEAD_DIM, tm // MOBA_BLOCK
    return pl.pallas_call(
        _mm_vt_kernel,
        grid=(m // tm, n // tn),
        in_specs=[pl.BlockSpec((tn, k), lambda i, j: (j, 0)), pl.BlockSpec((tm, k), lambda i, j: (i, 0))],
        out_specs=pl.BlockSpec((nh, nkb, HEAD_DIM, MOBA_BLOCK), lambda i, j: (j, i, 0, 0)),
        out_shape=jax.ShapeDtypeStruct((n // HEAD_DIM, m // MOBA_BLOCK, HEAD_DIM, MOBA_BLOCK), _BF16),
        compiler_params=_params("parallel", "parallel"),
        name=name,
    )(wt, x)


def _pool_kernel(halo_ref, u_ref, w_ref, scale_ref, o_ref, ext_ref, *, seq_tiles):
    tm = u_ref.shape[0]
    gw = w_ref.shape[1]
    first = pl.program_id(0) % seq_tiles == 0
    halo = halo_ref[...].astype(_F32)
    ext_ref[0:POOL_HALO, :] = jnp.where(first, 0.0, halo)
    ext_ref[POOL_HALO:, :] = u_ref[...].astype(_F32)
    pos = (pl.program_id(0) % seq_tiles) * tm + lax.broadcasted_iota(jnp.int32, (tm, gw), 0)
    for g, win in enumerate(POOL_WINDOWS):
        cols = slice(g * gw, (g + 1) * gw)
        u = ext_ref[POOL_HALO:, cols]
        s = u
        for d in range(1, win):
            s = s + ext_ref[POOL_HALO - d:POOL_HALO - d + tm, cols]
        cnt = jnp.minimum(pos + 1, win).astype(_F32)
        mixed = (s / cnt - u).astype(_BF16)
        y = _dot(mixed, w_ref[g]) * scale_ref[:, cols]
        o_ref[:, cols] = y.astype(o_ref.dtype)


def _pool_mixer(hp, w_pool, scale, *, seq, tm, name):
    m, width = hp.shape
    tm = min(tm, seq)
    seq_tiles = seq // tm
    halo_per_tile = tm // POOL_HALO
    return pl.pallas_call(
        functools.partial(_pool_kernel, seq_tiles=seq_tiles),
        grid=(m // tm,),
        in_specs=[
            pl.BlockSpec((POOL_HALO, width), lambda i: (jnp.maximum(i * halo_per_tile - 1, 0), 0)),
            pl.BlockSpec((tm, width), lambda i: (i, 0)),
            pl.BlockSpec(w_pool.shape, lambda i: (0, 0, 0)),
            pl.BlockSpec((1, width), lambda i: (0, 0)),
        ],
        out_specs=pl.BlockSpec((tm, width), lambda i: (i, 0)),
        out_shape=jax.ShapeDtypeStruct((m, width), _BF16),
        scratch_shapes=[pltpu.VMEM((tm + POOL_HALO, width), _F32)],
        compiler_params=_params("parallel"),
        name=name,
    )(hp, hp, w_pool, scale)


def _moba_kernel(q_ref, k_ref, vt_ref, o_ref, kmean_ref, bias_ref, m_ref, l_ref, alpha_ref, acc_ref, p_ref,
                 s_even_ref, s_odd_ref, *, heads):
    n = pl.program_id(2)
    nb = kmean_ref.shape[1]
    bq = q_ref.shape[0]
    head_cols = [slice(h * HEAD_DIM, (h + 1) * HEAD_DIM) for h in range(heads)]

    def key_rows(block):
        return pl.ds(pl.multiple_of(block * MOBA_BLOCK, MOBA_BLOCK), MOBA_BLOCK)

    @pl.when(n == 0)
    def _():
        def block_mean(j, carry):
            for h in range(heads):
                kb = k_ref[key_rows(j), head_cols[h]].astype(_F32)
                kmean_ref[h, pl.ds(j, 1), :] = jnp.mean(kb, axis=0, keepdims=True)
            return carry
        lax.fori_loop(0, nb, block_mean, 0)

    def scores(block, s_ref, keep=None):
        for h in range(heads):
            s = _dot_nt(k_ref[key_rows(block), head_cols[h]], q_ref[:, head_cols[h]])
            s_ref[h] = s if keep is None else jnp.where(keep, s, MASK_VALUE)

    def weigh_values(block):
        for h in range(heads):
            acc_ref[h] = alpha_ref[h] * acc_ref[h] + _dot(vt_ref[h, block], p_ref[h])

    def softmax_update(item, s_ref):
        bias_row = jnp.where(item == 0, 0, item + (BIAS_PAD_ROWS - 1))
        for h in range(heads):
            s = s_ref[h] + bias_ref[h, pl.ds(bias_row, 1), :]
            m_old = m_ref[h]
            m_new = jnp.maximum(m_old, jnp.max(s, axis=0, keepdims=True))
            alpha = jnp.exp(m_old - m_new)
            p = jnp.exp(s - m_new)
            m_ref[h] = m_new
            l_ref[h] = alpha * l_ref[h] + jnp.sum(p, axis=0, keepdims=True)
            alpha_ref[h] = alpha
            p_ref[h] = p.astype(p_ref.dtype)

    def step(item, s_cur_ref, s_next_ref):
        weigh_values(jnp.where(item == 1, n, jnp.maximum(item - 2, 0)))
        scores(jnp.minimum(item, jnp.maximum(n - 1, 0)), s_next_ref)
        softmax_update(item, s_cur_ref)

    blk = lax.broadcasted_iota(jnp.int32, (nb, bq), 0).astype(_F32)
    n_f = n.astype(_F32)
    for h in range(heads):
        q = q_ref[:, head_cols[h]]
        km = kmean_ref[h]
        km_hi = km.astype(_BF16)
        km_lo = (km - km_hi.astype(_F32)).astype(_BF16)
        g = jnp.where(blk < n_f, _dot_nt(km_hi, q) + _dot_nt(km_lo, q), -jnp.inf)
        bias = jnp.full((nb, bq), MASK_VALUE, _F32)
        for _ in range(MOBA_TOPK):
            top = jnp.max(g, axis=0, keepdims=True)
            is_top = (g == top) & (g > -jnp.inf)
            first = jnp.min(jnp.where(is_top, blk, float(nb)), axis=0, keepdims=True)
            pick = blk == first
            bias = jnp.where(pick, 0.0, bias)
            g = jnp.where(pick, -jnp.inf, g)
        bias_ref[h, 0:BIAS_PAD_ROWS] = jnp.zeros((BIAS_PAD_ROWS, bq), _F32)
        bias_ref[h, BIAS_PAD_ROWS:] = bias
        m_ref[h] = jnp.full((1, bq), MASK_VALUE, _F32)
        l_ref[h] = jnp.zeros((1, bq), _F32)
        alpha_ref[h] = jnp.zeros((1, bq), _F32)
        acc_ref[h] = jnp.zeros(acc_ref.shape[1:], _F32)
        p_ref[h] = jnp.zeros(p_ref.shape[1:], p_ref.dtype)

    kidx = lax.broadcasted_iota(jnp.int32, (MOBA_BLOCK, bq), 0)
    qidx = lax.broadcasted_iota(jnp.int32, (MOBA_BLOCK, bq), 1)
    scores(n, s_even_ref, keep=kidx <= qidx)

    def two_steps(t, carry):
        step(2 * t, s_even_ref, s_odd_ref)
        step(2 * t + 1, s_odd_ref, s_even_ref)
        return carry
    lax.fori_loop(0, lax.shift_right_logical(n + 1, 1), two_steps, 0)

    @pl.when(lax.bitwise_and(n, 1) == 0)
    def _():
        step(n, s_even_ref, s_odd_ref)

    weigh_values(jnp.maximum(n - 1, 0))
    for h in range(heads):
        out_t = acc_ref[h] / l_ref[h]
        o_ref[:, head_cols[h]] = out_t.T.astype(o_ref.dtype)


def _moba_attention(qk, vt, *, batch, seq, attn_width, heads_per_step, name):
    t = qk.shape[0]
    nb = seq // MOBA_BLOCK
    gw = heads_per_step * HEAD_DIM
    n_groups = attn_width // gw
    return pl.pallas_call(
        functools.partial(_moba_kernel, heads=heads_per_step),
        grid=(batch, n_groups, nb),
        in_specs=[
            pl.BlockSpec((MOBA_BLOCK, gw), lambda b, g, n: (b * nb + n, g)),
            pl.BlockSpec((seq, gw), lambda b, g, n: (b, n_groups + g)),
            pl.BlockSpec((heads_per_step, nb, HEAD_DIM, MOBA_BLOCK), lambda b, g, n: (g, b, 0, 0)),
        ],
        out_specs=pl.BlockSpec((MOBA_BLOCK, gw), lambda b, g, n: (b * nb + n, g)),
        out_shape=jax.ShapeDtypeStruct((t, attn_width), _BF16),
        scratch_shapes=[
            pltpu.VMEM((heads_per_step, nb, HEAD_DIM), _F32),
            pltpu.VMEM((heads_per_step, BIAS_PAD_ROWS + nb, MOBA_BLOCK), _F32),
            pltpu.VMEM((heads_per_step, 1, MOBA_BLOCK), _F32),
            pltpu.VMEM((heads_per_step, 1, MOBA_BLOCK), _F32),
            pltpu.VMEM((heads_per_step, 1, MOBA_BLOCK), _F32),
            pltpu.VMEM((heads_per_step, HEAD_DIM, MOBA_BLOCK), _F32),
            pltpu.VMEM((heads_per_step, MOBA_BLOCK, MOBA_BLOCK), _BF16),
            pltpu.VMEM((heads_per_step, MOBA_BLOCK, MOBA_BLOCK), _F32),
            pltpu.VMEM((heads_per_step, MOBA_BLOCK, MOBA_BLOCK), _F32),
        ],
        compiler_params=_params("parallel", "parallel", "arbitrary"),
        name=name,
    )(qk, qk, vt)


def _mm_ln_kernel(a_ref, w_ref, r_ref, g_ref, b_ref, o_ref, *maybe_ob_ref, k_tiles, row_chunk, col_chunk):
    kk = pl.program_id(1)

    @pl.when(kk == 0)
    def _():
        o_ref[...] = DEEPNORM_ALPHA * r_ref[...]

    a = a_ref[...]
    for c in range(o_ref.shape[1] // col_chunk):
        cols = slice(c * col_chunk, (c + 1) * col_chunk)
        o_ref[:, cols] += _dot(a, w_ref[:, cols])

    @pl.when(kk == k_tiles - 1)
    def _():
        gain = g_ref[...]
        shift = b_ref[...]
        for r in range(o_ref.shape[0] // row_chunk):
            rows = slice(r * row_chunk, (r + 1) * row_chunk)
            y = o_ref[rows, :]
            mu = jnp.mean(y, axis=-1, keepdims=True)
            yc = y - mu
            var = jnp.mean(yc * yc, axis=-1, keepdims=True)
            h = yc * lax.rsqrt(var + LN_EPS) * gain + shift
            o_ref[rows, :] = h
            for ob_ref in maybe_ob_ref:
                ob_ref[rows, :] = h.astype(ob_ref.dtype)


def _matmul_ln(a, w, resid, gain, shift, *, tm, tk, with_bf16, name):
    m, k = a.shape
    n = w.shape[1]
    tm, tk = min(tm, m), min(tk, k)
    k_tiles = k // tk
    out_shape = [jax.ShapeDtypeStruct((m, n), _F32)]
    out_specs = [pl.BlockSpec((tm, n), lambda i, kk: (i, 0))]
    if with_bf16:
        out_shape.append(jax.ShapeDtypeStruct((m, n), _BF16))
        out_specs.append(pl.BlockSpec((tm, n), lambda i, kk: (i, 0)))
    outs = pl.pallas_call(
        functools.partial(_mm_ln_kernel, k_tiles=k_tiles, row_chunk=min(64, tm), col_chunk=min(1024, n)),
        grid=(m // tm, k_tiles),
        in_specs=[
            pl.BlockSpec((tm, tk), lambda i, kk: (i, kk)),
            pl.BlockSpec((tk, n), lambda i, kk: (kk, 0)),
            pl.BlockSpec((tm, n), lambda i, kk: (i, 0), pipeline_mode=pl.Buffered(1)),
            pl.BlockSpec((1, n), lambda i, kk: (0, 0)),
            pl.BlockSpec((1, n), lambda i, kk: (0, 0)),
        ],
        out_specs=out_specs,
        out_shape=out_shape,
        compiler_params=_params("parallel", "arbitrary"),
        name=name,
    )(a, w, resid, gain, shift)
    return outs if with_bf16 else outs[0]


def _xattn_kernel(h_ref, wq_ref, k_ref, v_ref, o_ref, *, scale):
    q = _dot(h_ref[...], wq_ref[...]).astype(_BF16)
    s = _dot_nt(q, k_ref[...]) * scale
    m = jnp.max(s, axis=-1, keepdims=True)
    p = jnp.exp(s - m)
    l = jnp.sum(p, axis=-1, keepdims=True)
    o = _dot(p.astype(_BF16), v_ref[...]) / l
    o_ref[...] = o.astype(o_ref.dtype)


def _cross_attention(h, wq, kv, *, seq, mem_len, tm, name):
    t, d = h.shape
    hd = d // XATTN_HEADS
    tm = min(tm, seq)
    seq_tiles = seq // tm
    return pl.pallas_call(
        functools.partial(_xattn_kernel, scale=hd ** -0.5),
        grid=(t // tm, XATTN_HEADS),
        in_specs=[
            pl.BlockSpec((tm, d), lambda i, hh: (i, 0)),
            pl.BlockSpec((d, hd), lambda i, hh: (0, hh)),
            pl.BlockSpec((mem_len, hd), lambda i, hh: (i // seq_tiles, hh)),
            pl.BlockSpec((mem_len, hd), lambda i, hh: (i // seq_tiles, XATTN_HEADS + hh)),
        ],
        out_specs=pl.BlockSpec((tm, hd), lambda i, hh: (i, hh)),
        out_shape=jax.ShapeDtypeStruct((t, d), _BF16),
        compiler_params=_params("parallel", "parallel"),
        name=name,
    )(h, wq, kv, kv)


def _gate_up_kernel(x_ref, wg_ref, wu_ref, o_ref):
    x = x_ref[...]
    g = _dot(x, wg_ref[...])
    u = _dot(x, wu_ref[...])
    o_ref[...] = (g * jax.nn.sigmoid(g) * u).astype(o_ref.dtype)


def _gate_up(x, wg, wu, *, tm, tn, name):
    m, k = x.shape
    n = wg.shape[1]
    tm, tn = min(tm, m), min(tn, n)
    return pl.pallas_call(
        _gate_up_kernel,
        grid=(m // tm, n // tn),
        in_specs=[
            pl.BlockSpec((tm, k), lambda i, j: (i, 0)),
            pl.BlockSpec((k, tn), lambda i, j: (0, j)),
            pl.BlockSpec((k, tn), lambda i, j: (0, j)),
        ],
        out_specs=pl.BlockSpec((tm, tn), lambda i, j: (i, j)),
        out_shape=jax.ShapeDtypeStruct((m, n), _BF16),
        compiler_params=_params("parallel", "parallel"),
        name=name,
    )(x, wg, wu)


def _rope_tables(seq):
    half = HEAD_DIM // 2
    inv = ROPE_THETA ** (-jnp.arange(half, dtype=_F32) / half)
    ang = jnp.arange(seq, dtype=jnp.int32).astype(_F32)[:, None] * inv[None, :]
    cos, sin = jnp.cos(ang), jnp.sin(ang)
    return jnp.concatenate([cos, cos], axis=-1), jnp.concatenate([-sin, sin], axis=-1)


def _layer(h, hb, mem_b, w_in, w_pool, pool_scale, w_out, ln1_g, ln1_b, w_xq, w_xkv, w_xo, ln2_g, ln2_b,
           w_gate, w_up, w_down, ln3_g, ln3_b, *, batch, seq):
    d = h.shape[1]
    pool_width = w_pool.shape[0] * w_pool.shape[1]
    attn_width = (w_in.shape[1] - pool_width) // 3
    mem_len = mem_b.shape[0] // batch
    hidden = w_gate.shape[1]
    hidden_pad = -(-hidden // FFN_TILE) * FFN_TILE

    w_p = w_in[:, :pool_width].astype(_BF16)
    w_qk = w_in[:, pool_width:pool_width + 2 * attn_width].astype(_BF16)
    w_vt = w_in[:, pool_width + 2 * attn_width:].T.astype(_BF16)
    pad = ((0, 0), (0, hidden_pad - hidden))
    w_gate_b = jnp.pad(w_gate.astype(_BF16), pad)
    w_up_b = jnp.pad(w_up.astype(_BF16), pad)
    w_down_b = jnp.pad(w_down.astype(_BF16), (pad[1], pad[0]))
    cos, sin = _rope_tables(seq)
    row = lambda v: v.reshape(1, -1)

    hp = _matmul(hb, w_p, tm=1024, tn=1024, out_dtype=_BF16, name="in_proj_pool")
    qk = _matmul_rope(hb, w_qk, cos, sin, seq=seq, q_width=attn_width, q_scale=HEAD_DIM ** -0.5,
                      tm=1024, tn=1024, name="in_proj_qk_rope")
    vt = _matmul_vt(w_vt, hb, tm=1024, tn=1024, name="in_proj_vt")
    pool_out = _pool_mixer(hp, w_pool.astype(_BF16), row(pool_scale), seq=seq, tm=512, name="pool_mixer")
    attn = _moba_attention(qk, vt, batch=batch, seq=seq, attn_width=attn_width, heads_per_step=4,
                           name="moba_attention")
    mix_in = jnp.concatenate([pool_out, attn], axis=-1)
    h1, h1b = _matmul_ln(mix_in, w_out.astype(_BF16), h, row(ln1_g), row(ln1_b), tm=512, tk=512,
                         with_bf16=True, name="mix_out_ln1")

    kv = _matmul(mem_b, w_xkv.astype(_BF16), tm=512, tn=1024, out_dtype=_BF16, name="xattn_kv")
    xo = _cross_attention(h1b, w_xq.astype(_BF16), kv, seq=seq, mem_len=mem_len, tm=1024, name="xattn_q_attend")
    h2, h2b = _matmul_ln(xo, w_xo.astype(_BF16), h1, row(ln2_g), row(ln2_b), tm=512, tk=512,
                         with_bf16=True, name="xattn_out_ln2")

    act = _gate_up(h2b, w_gate_b, w_up_b, tm=1024, tn=FFN_TILE, name="ffn_gate_up")
    h3 = _matmul_ln(act, w_down_b, h2, row(ln3_g), row(ln3_b), tm=512, tk=FFN_TILE,
                    with_bf16=False, name="ffn_down_ln3")
    return h3


def kernel(x, mem, w_mix_in, w_pool, pool_scale, w_mix_out, ln1_g, ln1_b, w_xq, w_xkv, w_xo, ln2_g, ln2_b,
           w_gate, w_up, w_down, ln3_g, ln3_b):
    batch, seq, d = x.shape
    assert w_mix_in.shape[0] == DEPTH
    h = x.reshape(batch * seq, d)
    mem_b = mem.reshape(-1, d).astype(_BF16)
    for l in range(DEPTH):
        h = _layer(h, h.astype(_BF16), mem_b, w_mix_in[l], w_pool[l], pool_scale[l], w_mix_out[l],
                   ln1_g[l], ln1_b[l], w_xq[l], w_xkv[l], w_xo[l], ln2_g[l], ln2_b[l],
                   w_gate[l], w_up[l], w_down[l], ln3_g[l], ln3_b[l], batch=batch, seq=seq)
    return h.reshape(batch, seq, d)
```

```python
import functools

import jax
import jax.numpy as jnp
from jax import lax
from jax.experimental import pallas as pl
from jax.experimental.pallas import tpu as pltpu

POOL_WINDOWS = (2, 4, 8, 16)
HEAD_DIM = 128
MOBA_BLOCK = 256
MOBA_TOPK = 3
ROPE_THETA = 10000.0
XATTN_HEADS = 4
LN_EPS = 1e-5
DEPTH = 1
DEEPNORM_ALPHA = (2 * DEPTH) ** 0.25

V7X_LANES = 128
V7X_VMEM_LIMIT_BYTES = 56 * 1024 * 1024
MASK_VALUE = -1e30
LOG2_E = 1.4426950408889634
VT_ROWS = HEAD_DIM + 16
POOL_HALO = 16
FFN_TILE = 512

_F32 = jnp.float32
_BF16 = jnp.bfloat16
_NT = (((1,), (1,)), ((), ()))


def _params(*semantics):
    return pltpu.CompilerParams(dimension_semantics=semantics, vmem_limit_bytes=V7X_VMEM_LIMIT_BYTES)


def _dot(a, b):
    return jnp.dot(a, b, preferred_element_type=_F32)


def _dot_nt(a, b):
    return lax.dot_general(a, b, _NT, preferred_element_type=_F32)


def _mm_kernel(x_ref, w_ref, o_ref):
    o_ref[...] = _dot(x_ref[...], w_ref[...]).astype(o_ref.dtype)


def _matmul(x, w, *, tm, tn, out_dtype, name):
    m, k = x.shape
    n = w.shape[1]
    tm, tn = min(tm, m), min(tn, n)
    return pl.pallas_call(
        _mm_kernel,
        grid=(m // tm, n // tn),
        in_specs=[pl.BlockSpec((tm, k), lambda i, j: (i, 0)), pl.BlockSpec((k, tn), lambda i, j: (0, j))],
        out_specs=pl.BlockSpec((tm, tn), lambda i, j: (i, j)),
        out_shape=jax.ShapeDtypeStruct((m, n), out_dtype),
        compiler_params=_params("parallel", "parallel"),
        name=name,
    )(x, w)


def _mm_cast_kernel(x_ref, w_ref, o_ref, xb_ref):
    @pl.when(pl.program_id(1) == 0)
    def _():
        xb_ref[...] = x_ref[...].astype(xb_ref.dtype)

    o_ref[...] = _dot(xb_ref[...], w_ref[...]).astype(o_ref.dtype)


def _matmul_cast(x, w, *, tm, tn, out_dtype, name):
    m, k = x.shape
    n = w.shape[1]
    tm, tn = min(tm, m), min(tn, n)
    return pl.pallas_call(
        _mm_cast_kernel,
        grid=(m // tm, n // tn),
        in_specs=[pl.BlockSpec((tm, k), lambda i, j: (i, 0)), pl.BlockSpec((k, tn), lambda i, j: (0, j))],
        out_specs=[pl.BlockSpec((tm, tn), lambda i, j: (i, j)), pl.BlockSpec((tm, k), lambda i, j: (i, 0))],
        out_shape=[jax.ShapeDtypeStruct((m, n), out_dtype), jax.ShapeDtypeStruct((m, k), _BF16)],
        compiler_params=_params("parallel", "arbitrary"),
        name=name,
    )(x, w)


def _mm_rope_kernel(x_ref, w_ref, cos_ref, sin_ref, o_ref, *, q_tiles, q_scale):
    acc = _dot(x_ref[...], w_ref[...])
    scale = jnp.where(pl.program_id(1) < q_tiles, q_scale, 1.0).astype(_F32)
    cos = cos_ref[...] * scale
    sin = sin_ref[...] * scale
    for h in range(acc.shape[1] // HEAD_DIM):
        cols = slice(h * HEAD_DIM, (h + 1) * HEAD_DIM)
        t = acc[:, cols]
        o_ref[:, cols] = (t * cos + pltpu.roll(t, HEAD_DIM // 2, 1) * sin).astype(o_ref.dtype)


def _matmul_rope(x, w, cos, sin, *, seq, q_width, q_scale, tm, tn, name):
    m, k = x.shape
    n = w.shape[1]
    tm, tn = min(tm, seq), min(tn, n)
    seq_tiles = seq // tm
    kern = functools.partial(_mm_rope_kernel, q_tiles=q_width // tn, q_scale=q_scale)
    return pl.pallas_call(
        kern,
        grid=(m // tm, n // tn),
        in_specs=[
            pl.BlockSpec((tm, k), lambda i, j: (i, 0)),
            pl.BlockSpec((k, tn), lambda i, j: (0, j)),
            pl.BlockSpec((tm, HEAD_DIM), lambda i, j: (i % seq_tiles, 0)),
            pl.BlockSpec((tm, HEAD_DIM), lambda i, j: (i % seq_tiles, 0)),
        ],
        out_specs=pl.BlockSpec((tm, tn), lambda i, j: (i, j)),
        out_shape=jax.ShapeDtypeStruct((m, n), _BF16),
        compiler_params=_params("parallel", "parallel"),
        name=name,
    )(x, w, cos, sin)


def _mm_vt_kernel(wt_ref, x_ref, o_ref):
    acc = _dot_nt(wt_ref[...], x_ref[...]).astype(o_ref.dtype)
    ones = jnp.ones((VT_ROWS - HEAD_DIM, MOBA_BLOCK), o_ref.dtype)
    for h in range(o_ref.shape[0]):
        for kb in range(o_ref.shape[1]):
            o_ref[h, kb, :HEAD_DIM, :] = acc[h * HEAD_DIM:(h + 1) * HEAD_DIM, kb * MOBA_BLOCK:(kb + 1) * MOBA_BLOCK]
            o_ref[h, kb, HEAD_DIM:, :] = ones


def _matmul_vt(wt, x, *, tm, tn, name):
    n, k = wt.shape
    m = x.shape[0]
    tm, tn = min(tm, m), min(tn, n)
    nh, nkb = tn // HEAD_DIM, tm // MOBA_BLOCK
    return pl.pallas_call(
        _mm_vt_kernel,
        grid=(m // tm, n // tn),
        in_specs=[pl.BlockSpec((tn, k), lambda i, j: (j, 0)), pl.BlockSpec((tm, k), lambda i, j: (i, 0))],
        out_specs=pl.BlockSpec((nh, nkb, VT_ROWS, MOBA_BLOCK), lambda i, j: (j, i, 0, 0)),
        out_shape=jax.ShapeDtypeStruct((n // HEAD_DIM, m // MOBA_BLOCK, VT_ROWS, MOBA_BLOCK), _BF16),
        compiler_params=_params("parallel", "parallel"),
        name=name,
    )(wt, x)


def _pool_kernel(halo_ref, u_ref, w_ref, scale_ref, o_ref, ext_ref, *, seq_tiles):
    tm = u_ref.shape[0]
    gw = w_ref.shape[1]
    first = pl.program_id(0) % seq_tiles == 0
    halo = halo_ref[...].astype(_F32)
    ext_ref[0:POOL_HALO, :] = jnp.where(first, 0.0, halo)
    ext_ref[POOL_HALO:, :] = u_ref[...].astype(_F32)
    pos = (pl.program_id(0) % seq_tiles) * tm + lax.broadcasted_iota(jnp.int32, (tm, gw), 0)
    for g, win in enumerate(POOL_WINDOWS):
        cols = slice(g * gw, (g + 1) * gw)
        u = ext_ref[POOL_HALO:, cols]
        s = u
        for d in range(1, win):
            s = s + ext_ref[POOL_HALO - d:POOL_HALO - d + tm, cols]
        cnt = jnp.minimum(pos + 1, win).astype(_F32)
        mixed = (s / cnt - u).astype(_BF16)
        y = _dot(mixed, w_ref[g]) * scale_ref[:, cols]
        o_ref[:, cols] = y.astype(o_ref.dtype)


def _pool_mixer(hp, w_pool, scale, *, seq, tm, name):
    m, width = hp.shape
    tm = min(tm, seq)
    seq_tiles = seq // tm
    halo_per_tile = tm // POOL_HALO
    return pl.pallas_call(
        functools.partial(_pool_kernel, seq_tiles=seq_tiles),
        grid=(m // tm,),
        in_specs=[
            pl.BlockSpec((POOL_HALO, width), lambda i: (jnp.maximum(i * halo_per_tile - 1, 0), 0)),
            pl.BlockSpec((tm, width), lambda i: (i, 0)),
            pl.BlockSpec(w_pool.shape, lambda i: (0, 0, 0)),
            pl.BlockSpec((1, width), lambda i: (0, 0)),
        ],
        out_specs=pl.BlockSpec((tm, width), lambda i: (i, 0)),
        out_shape=jax.ShapeDtypeStruct((m, width), _BF16),
        scratch_shapes=[pltpu.VMEM((tm + POOL_HALO, width), _F32)],
        compiler_params=_params("parallel"),
        name=name,
    )(hp, hp, w_pool, scale)


def _moba_kernel(q_ref, k_ref, vt_ref, o_ref, kmean_ref, bias_ref, m_ref, alpha_ref, acc_ref, p_ref,
                 s_even_ref, s_odd_ref, *, heads):
    n = pl.program_id(2)
    nb = kmean_ref.shape[1]
    bq = q_ref.shape[0]
    head_cols = [slice(h * HEAD_DIM, (h + 1) * HEAD_DIM) for h in range(heads)]

    def key_rows(block):
        return pl.ds(pl.multiple_of(block * MOBA_BLOCK, MOBA_BLOCK), MOBA_BLOCK)

    @pl.when(n == 0)
    def _():
        def block_mean(j, carry):
            for h in range(heads):
                kb = k_ref[key_rows(j), head_cols[h]].astype(_F32)
                kmean_ref[h, pl.ds(j, 1), :] = jnp.mean(kb, axis=0, keepdims=True)
            return carry
        lax.fori_loop(0, nb, block_mean, 0)

    def scores(block, s_ref, bias_row=None, keep=None):
        for h in range(heads):
            s = _dot_nt(k_ref[key_rows(block), head_cols[h]], q_ref[:, head_cols[h]])
            if keep is not None:
                s_ref[h] = jnp.where(keep, s, MASK_VALUE)
            else:
                s_ref[h] = s + bias_ref[h, pl.ds(bias_row, 1), :]

    def weigh_values(block):
        for h in range(heads):
            acc_ref[h] = alpha_ref[h] * acc_ref[h] + _dot(vt_ref[h, block], p_ref[h])

    def softmax_update(s_ref):
        for h in range(heads):
            m_old = m_ref[h]
            m_new = jnp.maximum(m_old, jnp.max(s_ref[h], axis=0, keepdims=True))
            m_ref[h] = m_new
            alpha_ref[h] = jnp.exp2(m_old - m_new)
            p_ref[h] = jnp.exp2(s_ref[h] - m_new).astype(p_ref.dtype)

    def step(item, s_cur_ref, s_next_ref):
        weigh_values(jnp.where(item == 1, n, jnp.maximum(item - 2, 0)))
        scores(jnp.minimum(item, jnp.maximum(n - 1, 0)), s_next_ref, bias_row=item)
        softmax_update(s_cur_ref)

    blk = lax.broadcasted_iota(jnp.int32, (nb, bq), 0).astype(_F32)
    n_f = n.astype(_F32)
    for h in range(heads):
        q = q_ref[:, head_cols[h]]
        km = kmean_ref[h]
        km_hi = km.astype(_BF16)
        km_lo = (km - km_hi.astype(_F32)).astype(_BF16)
        g = jnp.where(blk < n_f, _dot_nt(km_hi, q) + _dot_nt(km_lo, q), -jnp.inf)
        bias = jnp.full((nb, bq), MASK_VALUE, _F32)
        for _ in range(MOBA_TOPK):
            top = jnp.max(g, axis=0, keepdims=True)
            is_top = (g == top) & (g > -jnp.inf)
            first = jnp.min(jnp.where(is_top, blk, float(nb)), axis=0, keepdims=True)
            pick = blk == first
            bias = jnp.where(pick, 0.0, bias)
            g = jnp.where(pick, -jnp.inf, g)
        bias_ref[h] = bias
        m_ref[h] = jnp.full((1, bq), MASK_VALUE, _F32)
        alpha_ref[h] = jnp.zeros((1, bq), _F32)
        acc_ref[h] = jnp.zeros(acc_ref.shape[1:], _F32)
        p_ref[h] = jnp.zeros(p_ref.shape[1:], p_ref.dtype)

    kidx = lax.broadcasted_iota(jnp.int32, (MOBA_BLOCK, bq), 0)
    qidx = lax.broadcasted_iota(jnp.int32, (MOBA_BLOCK, bq), 1)
    scores(n, s_even_ref, keep=kidx <= qidx)

    def two_steps(t, carry):
        step(2 * t, s_even_ref, s_odd_ref)
        step(2 * t + 1, s_odd_ref, s_even_ref)
        return carry
    lax.fori_loop(0, lax.shift_right_logical(n + 1, 1), two_steps, 0)

    @pl.when(lax.bitwise_and(n, 1) == 0)
    def _():
        step(n, s_even_ref, s_odd_ref)

    weigh_values(jnp.maximum(n - 1, 0))
    for h in range(heads):
        out_t = acc_ref[h, :HEAD_DIM, :] / acc_ref[h, HEAD_DIM:HEAD_DIM + 1, :]
        o_ref[:, head_cols[h]] = out_t.T.astype(o_ref.dtype)


def _moba_attention(qk, vt, *, batch, seq, attn_width, heads_per_step, name):
    t = qk.shape[0]
    nb = seq // MOBA_BLOCK
    gw = heads_per_step * HEAD_DIM
    n_groups = attn_width // gw
    return pl.pallas_call(
        functools.partial(_moba_kernel, heads=heads_per_step),
        grid=(batch, n_groups, nb),
        in_specs=[
            pl.BlockSpec((MOBA_BLOCK, gw), lambda b, g, n: (b * nb + n, g)),
            pl.BlockSpec((seq, gw), lambda b, g, n: (b, n_groups + g)),
            pl.BlockSpec((heads_per_step, nb, VT_ROWS, MOBA_BLOCK), lambda b, g, n: (g, b, 0, 0)),
        ],
        out_specs=pl.BlockSpec((MOBA_BLOCK, gw), lambda b, g, n: (b * nb + n, g)),
        out_shape=jax.ShapeDtypeStruct((t, attn_width), _BF16),
        scratch_shapes=[
            pltpu.VMEM((heads_per_step, nb, HEAD_DIM), _F32),
            pltpu.VMEM((heads_per_step, nb, MOBA_BLOCK), _F32),
            pltpu.VMEM((heads_per_step, 1, MOBA_BLOCK), _F32),
            pltpu.VMEM((heads_per_step, 1, MOBA_BLOCK), _F32),
            pltpu.VMEM((heads_per_step, VT_ROWS, MOBA_BLOCK), _F32),
            pltpu.VMEM((heads_per_step, MOBA_BLOCK, MOBA_BLOCK), _BF16),
            pltpu.VMEM((heads_per_step, MOBA_BLOCK, MOBA_BLOCK), _F32),
            pltpu.VMEM((heads_per_step, MOBA_BLOCK, MOBA_BLOCK), _F32),
        ],
        compiler_params=_params("parallel", "parallel", "arbitrary"),
        name=name,
    )(qk, qk, vt)


def _ln_kernel(y_ref, g_ref, b_ref, o_ref, *maybe_stats_ref):
    y = y_ref[...]
    mu = jnp.mean(y, axis=-1, keepdims=True)
    yc = y - mu
    var = jnp.mean(yc * yc, axis=-1, keepdims=True)
    rstd = lax.rsqrt(var + LN_EPS)
    o_ref[...] = (yc * rstd * g_ref[...] + b_ref[...]).astype(o_ref.dtype)
    for stats_ref in maybe_stats_ref:
        stats_ref[:, :V7X_LANES] = jnp.broadcast_to(mu, (y.shape[0], V7X_LANES))
        stats_ref[:, V7X_LANES:] = jnp.broadcast_to(rstd, (y.shape[0], V7X_LANES))


def _layer_norm(y, gain, shift, *, out_dtype, with_stats, tm, name):
    m, n = y.shape
    tm = min(tm, m)
    out_shape = [jax.ShapeDtypeStruct((m, n), out_dtype)]
    out_specs = [pl.BlockSpec((tm, n), lambda i: (i, 0))]
    if with_stats:
        out_shape.append(jax.ShapeDtypeStruct((m, 2 * V7X_LANES), _F32))
        out_specs.append(pl.BlockSpec((tm, 2 * V7X_LANES), lambda i: (i, 0)))
    outs = pl.pallas_call(
        _ln_kernel,
        grid=(m // tm,),
        in_specs=[
            pl.BlockSpec((tm, n), lambda i: (i, 0)),
            pl.BlockSpec((1, n), lambda i: (0, 0)),
            pl.BlockSpec((1, n), lambda i: (0, 0)),
        ],
        out_specs=out_specs,
        out_shape=out_shape,
        compiler_params=_params("parallel"),
        name=name,
    )(y, gain, shift)
    return outs if with_stats else outs[0]


def _mm_resid_kernel(*refs, n_lhs, ln_resid):
    lhs_refs, w_ref, r_ref, o_ref = refs[:n_lhs], refs[n_lhs], refs[n_lhs + 1], refs[-1]
    acc, k0 = None, 0
    for a_ref in lhs_refs:
        part = _dot(a_ref[...], w_ref[k0:k0 + a_ref.shape[1], :])
        acc = part if acc is None else acc + part
        k0 += a_ref.shape[1]
    if not ln_resid:
        o_ref[...] = DEEPNORM_ALPHA * r_ref[...] + acc
        return
    stats_ref, g_ref, b_ref = refs[n_lhs + 2:n_lhs + 5]
    mu = stats_ref[:, :V7X_LANES]
    rstd = stats_ref[:, V7X_LANES:]
    for c in range(o_ref.shape[1] // V7X_LANES):
        cols = slice(c * V7X_LANES, (c + 1) * V7X_LANES)
        h = (r_ref[:, cols] - mu) * rstd * g_ref[:, cols] + b_ref[:, cols]
        o_ref[:, cols] = DEEPNORM_ALPHA * h + acc[:, cols]


def _matmul_resid(lhs, w, resid, ln=None, *, tm, tn, lhs_buffers, name):
    m = lhs[0].shape[0]
    n = w.shape[1]
    tm, tn = min(tm, m), min(tn, n)
    in_specs = [pl.BlockSpec((tm, a.shape[1]), lambda i, j: (i, 0), pipeline_mode=pl.Buffered(lhs_buffers))
                for a in lhs]
    in_specs += [pl.BlockSpec((w.shape[0], tn), lambda i, j: (0, j)), pl.BlockSpec((tm, tn), lambda i, j: (i, j))]
    operands = [*lhs, w, resid]
    if ln is not None:
        in_specs += [pl.BlockSpec((tm, 2 * V7X_LANES), lambda i, j: (i, 0)),
                     pl.BlockSpec((1, tn), lambda i, j: (0, j)), pl.BlockSpec((1, tn), lambda i, j: (0, j))]
        operands += list(ln)
    return pl.pallas_call(
        functools.partial(_mm_resid_kernel, n_lhs=len(lhs), ln_resid=ln is not None),
        grid=(m // tm, n // tn),
        in_specs=in_specs,
        out_specs=pl.BlockSpec((tm, tn), lambda i, j: (i, j)),
        out_shape=jax.ShapeDtypeStruct((m, n), _F32),
        compiler_params=_params("parallel", "parallel"),
        name=name,
    )(*operands)


def _xattn_kernel(h_ref, wq_ref, k_ref, v_ref, o_ref, *, scale):
    q = _dot(h_ref[...], wq_ref[...]).astype(_BF16)
    s = _dot_nt(q, k_ref[...]) * scale
    m = jnp.max(s, axis=-1, keepdims=True)
    p = jnp.exp(s - m)
    l = jnp.sum(p, axis=-1, keepdims=True)
    o = _dot(p.astype(_BF16), v_ref[...]) / l
    o_ref[...] = o.astype(o_ref.dtype)


def _cross_attention(h, wq, kv, *, seq, mem_len, tm, name):
    t, d = h.shape
    hd = d // XATTN_HEADS
    tm = min(tm, seq)
    seq_tiles = seq // tm
    return pl.pallas_call(
        functools.partial(_xattn_kernel, scale=hd ** -0.5),
        grid=(t // tm, XATTN_HEADS),
        in_specs=[
            pl.BlockSpec((tm, d), lambda i, hh: (i, 0)),
            pl.BlockSpec((d, hd), lambda i, hh: (0, hh)),
            pl.BlockSpec((mem_len, hd), lambda i, hh: (i // seq_tiles, hh)),
            pl.BlockSpec((mem_len, hd), lambda i, hh: (i // seq_tiles, XATTN_HEADS + hh)),
        ],
        out_specs=pl.BlockSpec((tm, hd), lambda i, hh: (i, hh)),
        out_shape=jax.ShapeDtypeStruct((t, d), _BF16),
        compiler_params=_params("parallel", "parallel"),
        name=name,
    )(h, wq, kv, kv)


def _gate_up_kernel(x_ref, wg_ref, wu_ref, o_ref):
    x = x_ref[...]
    g = _dot(x, wg_ref[...])
    u = _dot(x, wu_ref[...])
    o_ref[...] = (g * jax.nn.sigmoid(g) * u).astype(o_ref.dtype)


def _gate_up(x, wg, wu, *, tm, tn, name):
    m, k = x.shape
    n = wg.shape[1]
    tm, tn = min(tm, m), min(tn, n)
    return pl.pallas_call(
        _gate_up_kernel,
        grid=(m // tm, n // tn),
        in_specs=[
            pl.BlockSpec((tm, k), lambda i, j: (i, 0)),
            pl.BlockSpec((k, tn), lambda i, j: (0, j)),
            pl.BlockSpec((k, tn), lambda i, j: (0, j)),
        ],
        out_specs=pl.BlockSpec((tm, tn), lambda i, j: (i, j)),
        out_shape=jax.ShapeDtypeStruct((m, n), _BF16),
        compiler_params=_params("parallel", "parallel"),
        name=name,
    )(x, wg, wu)


def _rope_tables(seq):
    half = HEAD_DIM // 2
    inv = ROPE_THETA ** (-jnp.arange(half, dtype=_F32) / half)
    ang = jnp.arange(seq, dtype=jnp.int32).astype(_F32)[:, None] * inv[None, :]
    cos, sin = jnp.cos(ang), jnp.sin(ang)
    return jnp.concatenate([cos, cos], axis=-1), jnp.concatenate([-sin, sin], axis=-1)


def _layer(h, mem_b, w_in, w_pool, pool_scale, w_out, ln1_g, ln1_b, w_xq, w_xkv, w_xo, ln2_g, ln2_b,
           w_gate, w_up, w_down, ln3_g, ln3_b, *, batch, seq):
    pool_width = w_pool.shape[0] * w_pool.shape[1]
    attn_width = (w_in.shape[1] - pool_width) // 3
    mem_len = mem_b.shape[0] // batch
    hidden = w_gate.shape[1]
    hidden_pad = -(-hidden // FFN_TILE) * FFN_TILE

    w_p = w_in[:, :pool_width].astype(_BF16)
    w_qk = w_in[:, pool_width:pool_width + 2 * attn_width].astype(_BF16)
    w_vt = w_in[:, pool_width + 2 * attn_width:].T.astype(_BF16)
    pad = ((0, 0), (0, hidden_pad - hidden))
    w_gate_b = jnp.pad(w_gate, pad).astype(_BF16)
    w_up_b = jnp.pad(w_up, pad).astype(_BF16)
    w_down_b = jnp.pad(w_down, (pad[1], pad[0])).astype(_BF16)
    cos, sin = _rope_tables(seq)
    row = lambda v: v.reshape(1, -1)
    ln1, ln2, ln3 = (row(ln1_g), row(ln1_b)), (row(ln2_g), row(ln2_b)), (row(ln3_g), row(ln3_b))

    hp, hb = _matmul_cast(h, w_p, tm=512, tn=1024, out_dtype=_BF16, name="in_proj_pool")
    qk = _matmul_rope(hb, w_qk, cos, sin, seq=seq, q_width=attn_width, q_scale=HEAD_DIM ** -0.5 * LOG2_E,
                      tm=1024, tn=1024, name="in_proj_qk_rope")
    vt = _matmul_vt(w_vt, hb, tm=1024, tn=1024, name="in_proj_vt")
    pool_out = _pool_mixer(hp, w_pool.astype(_BF16), row(pool_scale), seq=seq, tm=512, name="pool_mixer")
    attn = _moba_attention(qk, vt, batch=batch, seq=seq, attn_width=attn_width, heads_per_step=4,
                           name="moba_attention")
    y1 = _matmul_resid([pool_out, attn], w_out.astype(_BF16), h, tm=1024, tn=512, lhs_buffers=2, name="mix_out")
    h1b, stats1 = _layer_norm(y1, *ln1, out_dtype=_BF16, with_stats=True, tm=256, name="ln1")

    kv = _matmul(mem_b, w_xkv.astype(_BF16), tm=512, tn=1024, out_dtype=_BF16, name="xattn_kv")
    xo = _cross_attention(h1b, w_xq.astype(_BF16), kv, seq=seq, mem_len=mem_len, tm=1024, name="xattn_q_attend")
    y2 = _matmul_resid([xo], w_xo.astype(_BF16), y1, (stats1, *ln1), tm=1024, tn=512, lhs_buffers=2,
                       name="xattn_out")
    h2b, stats2 = _layer_norm(y2, *ln2, out_dtype=_BF16, with_stats=True, tm=256, name="ln2")

    act = _gate_up(h2b, w_gate_b, w_up_b, tm=1024, tn=FFN_TILE, name="ffn_gate_up")
    y3 = _matmul_resid([act], w_down_b, y2, (stats2, *ln2), tm=1024, tn=256, lhs_buffers=1, name="ffn_down")
    return _layer_norm(y3, *ln3, out_dtype=_F32, with_stats=False, tm=256, name="ln3")


def kernel(x, mem, w_mix_in, w_pool, pool_scale, w_mix_out, ln1_g, ln1_b, w_xq, w_xkv, w_xo, ln2_g, ln2_b,
           w_gate, w_up, w_down, ln3_g, ln3_b):
    batch, seq, d = x.shape
    assert w_mix_in.shape[0] == DEPTH
    h = x.reshape(batch * seq, d)
    mem_b = mem.reshape(-1, d).astype(_BF16)
    for l in range(DEPTH):
        h = _layer(h, mem_b, w_mix_in[l], w_pool[l], pool_scale[l], w_mix_out[l],
                   ln1_g[l], ln1_b[l], w_xq[l], w_xkv[l], w_xo[l], ln2_g[l], ln2_b[l],
                   w_gate[l], w_up[l], w_down[l], ln3_g[l], ln3_b[l], batch=batch, seq=seq)
    return h.reshape(batch, seq, d)
```

```python
import functools

import jax
import jax.numpy as jnp
from jax import lax
from jax.experimental import pallas as pl
from jax.experimental.pallas import tpu as pltpu

POOL_WINDOWS = (2, 4, 8, 16)
HEAD_DIM = 128
MOBA_BLOCK = 256
MOBA_TOPK = 3
ROPE_THETA = 10000.0
XATTN_HEADS = 4
LN_EPS = 1e-5
DEPTH = 1
DEEPNORM_ALPHA = (2 * DEPTH) ** 0.25

V7X_LANES = 128
V7X_VMEM_LIMIT_BYTES = 56 * 1024 * 1024
MASK_VALUE = -1e30
LOG2_E = 1.4426950408889634
VT_ROWS = HEAD_DIM + 16
POOL_HALO = 16
FFN_TILE = 256

_F32 = jnp.float32
_BF16 = jnp.bfloat16
_NT = (((1,), (1,)), ((), ()))


def _params(*semantics):
    return pltpu.CompilerParams(dimension_semantics=semantics, vmem_limit_bytes=V7X_VMEM_LIMIT_BYTES)


def _dot(a, b):
    return jnp.dot(a, b, preferred_element_type=_F32)


def _dot_nt(a, b):
    return lax.dot_general(a, b, _NT, preferred_element_type=_F32)


def _w_spec(k, tn, col0, n):
    off = col0 // tn
    if n == tn:
        return pl.BlockSpec((k, tn), lambda i, j: (0, off), pipeline_mode=pl.Buffered(1))
    return pl.BlockSpec((k, tn), lambda i, j: (0, j + off))


def _mm_wcast_kernel(x_ref, w_ref, o_ref):
    o_ref[...] = _dot(x_ref[...], w_ref[...].astype(x_ref.dtype)).astype(o_ref.dtype)


def _matmul_wcast(x, w, *, tm, tn, out_dtype, name):
    m, k = x.shape
    n = w.shape[1]
    tm, tn = min(tm, m), min(tn, n)
    return pl.pallas_call(
        _mm_wcast_kernel,
        grid=(m // tm, n // tn),
        in_specs=[pl.BlockSpec((tm, k), lambda i, j: (i, 0)), pl.BlockSpec((k, tn), lambda i, j: (0, j))],
        out_specs=pl.BlockSpec((tm, tn), lambda i, j: (i, j)),
        out_shape=jax.ShapeDtypeStruct((m, n), out_dtype),
        compiler_params=_params("parallel", "parallel"),
        name=name,
    )(x, w)


def _mm_cast_kernel(x_ref, w_ref, o_ref, xb_ref):
    @pl.when(pl.program_id(1) == 0)
    def _():
        xb_ref[...] = x_ref[...].astype(xb_ref.dtype)

    o_ref[...] = _dot(xb_ref[...], w_ref[...]).astype(o_ref.dtype)


def _matmul_cast(x, w, *, col0, n, tm, tn, out_dtype, name):
    m, k = x.shape
    tm, tn = min(tm, m), min(tn, n)
    return pl.pallas_call(
        _mm_cast_kernel,
        grid=(m // tm, n // tn),
        in_specs=[pl.BlockSpec((tm, k), lambda i, j: (i, 0)), _w_spec(k, tn, col0, n)],
        out_specs=[pl.BlockSpec((tm, tn), lambda i, j: (i, j)), pl.BlockSpec((tm, k), lambda i, j: (i, 0))],
        out_shape=[jax.ShapeDtypeStruct((m, n), out_dtype), jax.ShapeDtypeStruct((m, k), _BF16)],
        compiler_params=_params("parallel", "arbitrary"),
        name=name,
    )(x, w)


def _mm_rope_kernel(x_ref, w_ref, cos_ref, sin_ref, o_ref, *, q_tiles, q_scale):
    acc = _dot(x_ref[...], w_ref[...])
    scale = jnp.where(pl.program_id(1) < q_tiles, q_scale, 1.0).astype(_F32)
    cos = cos_ref[...] * scale
    sin = sin_ref[...] * scale
    for h in range(acc.shape[1] // HEAD_DIM):
        cols = slice(h * HEAD_DIM, (h + 1) * HEAD_DIM)
        t = acc[:, cols]
        o_ref[:, cols] = (t * cos + pltpu.roll(t, HEAD_DIM // 2, 1) * sin).astype(o_ref.dtype)


def _matmul_rope(x, w, cos, sin, *, col0, n, seq, q_width, q_scale, tm, tn, name):
    m, k = x.shape
    tm, tn = min(tm, seq), min(tn, n)
    seq_tiles = seq // tm
    kern = functools.partial(_mm_rope_kernel, q_tiles=q_width // tn, q_scale=q_scale)
    return pl.pallas_call(
        kern,
        grid=(m // tm, n // tn),
        in_specs=[
            pl.BlockSpec((tm, k), lambda i, j: (i, 0)),
            _w_spec(k, tn, col0, n),
            pl.BlockSpec((tm, HEAD_DIM), lambda i, j: (i % seq_tiles, 0)),
            pl.BlockSpec((tm, HEAD_DIM), lambda i, j: (i % seq_tiles, 0)),
        ],
        out_specs=pl.BlockSpec((tm, tn), lambda i, j: (i, j)),
        out_shape=jax.ShapeDtypeStruct((m, n), _BF16),
        compiler_params=_params("parallel", "parallel"),
        name=name,
    )(x, w, cos, sin)


def _mm_vt_kernel(x_ref, w_ref, o_ref):
    acc = _dot(x_ref[...], w_ref[...])
    ones = jnp.ones((VT_ROWS - HEAD_DIM, MOBA_BLOCK), o_ref.dtype)
    for h in range(o_ref.shape[0]):
        for kb in range(o_ref.shape[1]):
            v = acc[kb * MOBA_BLOCK:(kb + 1) * MOBA_BLOCK, h * HEAD_DIM:(h + 1) * HEAD_DIM]
            o_ref[h, kb, :HEAD_DIM, :] = v.T.astype(o_ref.dtype)
            o_ref[h, kb, HEAD_DIM:, :] = ones


def _matmul_vt(x, w, *, col0, n, tm, tn, name):
    m, k = x.shape
    tm, tn = min(tm, m), min(tn, n)
    nh, nkb = tn // HEAD_DIM, tm // MOBA_BLOCK
    return pl.pallas_call(
        _mm_vt_kernel,
        grid=(m // tm, n // tn),
        in_specs=[pl.BlockSpec((tm, k), lambda i, j: (i, 0)), _w_spec(k, tn, col0, n)],
        out_specs=pl.BlockSpec((nh, nkb, VT_ROWS, MOBA_BLOCK), lambda i, j: (j, i, 0, 0)),
        out_shape=jax.ShapeDtypeStruct((n // HEAD_DIM, m // MOBA_BLOCK, VT_ROWS, MOBA_BLOCK), _BF16),
        compiler_params=_params("parallel", "parallel"),
        name=name,
    )(x, w)


def _pool_kernel(halo_ref, u_ref, w_ref, scale_ref, o_ref, ext_ref, *, seq_tiles):
    tm = u_ref.shape[0]
    gw = w_ref.shape[1]
    first = pl.program_id(0) % seq_tiles == 0
    halo = halo_ref[...].astype(_F32)
    ext_ref[0:POOL_HALO, :] = jnp.where(first, 0.0, halo)
    ext_ref[POOL_HALO:, :] = u_ref[...].astype(_F32)
    pos = (pl.program_id(0) % seq_tiles) * tm + lax.broadcasted_iota(jnp.int32, (tm, gw), 0)
    for g, win in enumerate(POOL_WINDOWS):
        cols = slice(g * gw, (g + 1) * gw)
        u = ext_ref[POOL_HALO:, cols]
        s = u
        for d in range(1, win):
            s = s + ext_ref[POOL_HALO - d:POOL_HALO - d + tm, cols]
        cnt = jnp.minimum(pos + 1, win).astype(_F32)
        mixed = (s / cnt - u).astype(_BF16)
        y = _dot(mixed, w_ref[g]) * scale_ref[:, cols]
        o_ref[:, cols] = y.astype(o_ref.dtype)


def _pool_mixer(hp, w_pool, scale, *, seq, tm, name):
    m, width = hp.shape
    tm = min(tm, seq)
    seq_tiles = seq // tm
    halo_per_tile = tm // POOL_HALO
    return pl.pallas_call(
        functools.partial(_pool_kernel, seq_tiles=seq_tiles),
        grid=(m // tm,),
        in_specs=[
            pl.BlockSpec((POOL_HALO, width), lambda i: (jnp.maximum(i * halo_per_tile - 1, 0), 0)),
            pl.BlockSpec((tm, width), lambda i: (i, 0)),
            pl.BlockSpec(w_pool.shape, lambda i: (0, 0, 0)),
            pl.BlockSpec((1, width), lambda i: (0, 0)),
        ],
        out_specs=pl.BlockSpec((tm, width), lambda i: (i, 0)),
        out_shape=jax.ShapeDtypeStruct((m, width), _BF16),
        scratch_shapes=[pltpu.VMEM((tm + POOL_HALO, width), _F32)],
        compiler_params=_params("parallel"),
        name=name,
    )(hp, hp, w_pool, scale)


def _moba_kernel(q_ref, k_ref, vt_ref, o_ref, kmean_ref, qt_ref, bias_ref, m_ref, alpha_ref, acc_ref, p_ref,
                 s_ref, *, heads):
    n = pl.program_id(2)
    nb = kmean_ref.shape[1]
    bq = q_ref.shape[0]
    head_cols = [slice(h * HEAD_DIM, (h + 1) * HEAD_DIM) for h in range(heads)]

    def key_rows(block):
        return pl.ds(pl.multiple_of(block * MOBA_BLOCK, MOBA_BLOCK), MOBA_BLOCK)

    @pl.when(n == 0)
    def _():
        def block_mean(j, carry):
            for h in range(heads):
                kb = k_ref[key_rows(j), head_cols[h]].astype(_F32)
                kmean_ref[h, pl.ds(j, 1), :] = jnp.mean(kb, axis=0, keepdims=True)
            return carry
        lax.fori_loop(0, nb, block_mean, 0)

    def scores(block, parity, bias_row=None, keep=None):
        for h in range(heads):
            s = _dot(k_ref[key_rows(block), head_cols[h]], qt_ref[h])
            if keep is not None:
                s_ref[parity, h] = jnp.where(keep, s, MASK_VALUE)
            else:
                s_ref[parity, h] = s + bias_ref[h, pl.ds(bias_row, 1), :]

    def weigh_values(block, parity):
        for h in range(heads):
            acc_ref[h] = alpha_ref[parity, h] * acc_ref[h] + _dot(vt_ref[h, block], p_ref[parity, h])

    def softmax_update(parity):
        for h in range(heads):
            m_old = m_ref[h]
            m_new = jnp.maximum(m_old, jnp.max(s_ref[parity, h], axis=0, keepdims=True))
            m_ref[h] = m_new
            alpha_ref[parity, h] = jnp.exp2(m_old - m_new)
            p_ref[parity, h] = jnp.exp2(s_ref[parity, h] - m_new).astype(p_ref.dtype)

    def step(item, parity):
        weigh_values(jnp.where(item == 1, n, jnp.maximum(item - 2, 0)), 1 - parity)
        scores(jnp.minimum(item, jnp.maximum(n - 1, 0)), 1 - parity, bias_row=item)
        softmax_update(parity)

    blk = lax.broadcasted_iota(jnp.int32, (nb, bq), 0).astype(_F32)
    n_f = n.astype(_F32)
    for h in range(heads):
        qt = q_ref[:, head_cols[h]].astype(_F32).T.astype(qt_ref.dtype)
        qt_ref[h] = qt
        km = kmean_ref[h]
        km_hi = km.astype(_BF16)
        km_lo = (km - km_hi.astype(_F32)).astype(_BF16)
        g = jnp.where(blk < n_f, _dot(km_hi, qt) + _dot(km_lo, qt), -jnp.inf)
        bias = jnp.full((nb, bq), MASK_VALUE, _F32)
        for _ in range(MOBA_TOPK):
            top = jnp.max(g, axis=0, keepdims=True)
            is_top = (g == top) & (g > -jnp.inf)
            first = jnp.min(jnp.where(is_top, blk, float(nb)), axis=0, keepdims=True)
            pick = blk == first
            bias = jnp.where(pick, 0.0, bias)
            g = jnp.where(pick, -jnp.inf, g)
        bias_ref[h] = bias
        m_ref[h] = jnp.full((1, bq), MASK_VALUE, _F32)
        alpha_ref[1, h] = jnp.zeros((1, bq), _F32)
        acc_ref[h] = jnp.zeros(acc_ref.shape[1:], _F32)
        p_ref[1, h] = jnp.zeros(p_ref.shape[2:], p_ref.dtype)

    kidx = lax.broadcasted_iota(jnp.int32, (MOBA_BLOCK, bq), 0)
    qidx = lax.broadcasted_iota(jnp.int32, (MOBA_BLOCK, bq), 1)
    scores(n, 0, keep=kidx <= qidx)

    def two_steps(t, carry):
        step(2 * t, 0)
        step(2 * t + 1, 1)
        return carry
    lax.fori_loop(0, lax.shift_right_logical(n + 1, 1), two_steps, 0)

    last_block = jnp.maximum(n - 1, 0)

    @pl.when(lax.bitwise_and(n, 1) == 0)
    def _():
        step(n, 0)
        weigh_values(last_block, 0)

    @pl.when(lax.bitwise_and(n, 1) == 1)
    def _():
        weigh_values(last_block, 1)

    for h in range(heads):
        out_t = acc_ref[h, :HEAD_DIM, :] / acc_ref[h, HEAD_DIM:HEAD_DIM + 1, :]
        o_ref[:, head_cols[h]] = out_t.T.astype(o_ref.dtype)


def _moba_attention(qk, vt, *, batch, seq, attn_width, heads_per_step, name):
    t = qk.shape[0]
    nb = seq // MOBA_BLOCK
    gw = heads_per_step * HEAD_DIM
    n_groups = attn_width // gw
    return pl.pallas_call(
        functools.partial(_moba_kernel, heads=heads_per_step),
        grid=(batch, n_groups, nb),
        in_specs=[
            pl.BlockSpec((MOBA_BLOCK, gw), lambda b, g, n: (b * nb + n, g)),
            pl.BlockSpec((seq, gw), lambda b, g, n: (b, n_groups + g)),
            pl.BlockSpec((heads_per_step, nb, VT_ROWS, MOBA_BLOCK), lambda b, g, n: (g, b, 0, 0)),
        ],
        out_specs=pl.BlockSpec((MOBA_BLOCK, gw), lambda b, g, n: (b * nb + n, g)),
        out_shape=jax.ShapeDtypeStruct((t, attn_width), _BF16),
        scratch_shapes=[
            pltpu.VMEM((heads_per_step, nb, HEAD_DIM), _F32),
            pltpu.VMEM((heads_per_step, HEAD_DIM, MOBA_BLOCK), _BF16),
            pltpu.VMEM((heads_per_step, nb, MOBA_BLOCK), _F32),
            pltpu.VMEM((heads_per_step, 1, MOBA_BLOCK), _F32),
            pltpu.VMEM((2, heads_per_step, 1, MOBA_BLOCK), _F32),
            pltpu.VMEM((heads_per_step, VT_ROWS, MOBA_BLOCK), _F32),
            pltpu.VMEM((2, heads_per_step, MOBA_BLOCK, MOBA_BLOCK), _BF16),
            pltpu.VMEM((2, heads_per_step, MOBA_BLOCK, MOBA_BLOCK), _F32),
        ],
        compiler_params=_params("parallel", "parallel", "arbitrary"),
        name=name,
    )(qk, qk, vt)


def _ln_kernel(y_ref, g_ref, b_ref, o_ref, *maybe_stats_ref):
    y = y_ref[...]
    mu = jnp.mean(y, axis=-1, keepdims=True)
    yc = y - mu
    var = jnp.mean(yc * yc, axis=-1, keepdims=True)
    rstd = lax.rsqrt(var + LN_EPS)
    o_ref[...] = (yc * rstd * g_ref[...] + b_ref[...]).astype(o_ref.dtype)
    for stats_ref in maybe_stats_ref:
        stats_ref[:, :V7X_LANES] = jnp.broadcast_to(mu, (y.shape[0], V7X_LANES))
        stats_ref[:, V7X_LANES:] = jnp.broadcast_to(rstd, (y.shape[0], V7X_LANES))


def _layer_norm(y, gain, shift, *, out_dtype, with_stats, tm, name):
    m, n = y.shape
    tm = min(tm, m)
    out_shape = [jax.ShapeDtypeStruct((m, n), out_dtype)]
    out_specs = [pl.BlockSpec((tm, n), lambda i: (i, 0))]
    if with_stats:
        out_shape.append(jax.ShapeDtypeStruct((m, 2 * V7X_LANES), _F32))
        out_specs.append(pl.BlockSpec((tm, 2 * V7X_LANES), lambda i: (i, 0)))
    outs = pl.pallas_call(
        _ln_kernel,
        grid=(m // tm,),
        in_specs=[
            pl.BlockSpec((tm, n), lambda i: (i, 0)),
            pl.BlockSpec((1, n), lambda i: (0, 0)),
            pl.BlockSpec((1, n), lambda i: (0, 0)),
        ],
        out_specs=out_specs,
        out_shape=out_shape,
        compiler_params=_params("parallel"),
        name=name,
    )(y, gain, shift)
    return outs if with_stats else outs[0]


def _mm_resid_kernel(*refs, n_lhs, ln_resid):
    lhs_refs, w_ref, r_ref, o_ref = refs[:n_lhs], refs[n_lhs], refs[n_lhs + 1], refs[-1]
    acc, k0 = None, 0
    for a_ref in lhs_refs:
        part = _dot(a_ref[...], w_ref[k0:k0 + a_ref.shape[1], :])
        acc = part if acc is None else acc + part
        k0 += a_ref.shape[1]
    if not ln_resid:
        o_ref[...] = DEEPNORM_ALPHA * r_ref[...] + acc
        return
    stats_ref, g_ref, b_ref = refs[n_lhs + 2:n_lhs + 5]
    mu = stats_ref[:, :V7X_LANES]
    rstd = stats_ref[:, V7X_LANES:]
    for c in range(o_ref.shape[1] // V7X_LANES):
        cols = slice(c * V7X_LANES, (c + 1) * V7X_LANES)
        h = (r_ref[:, cols] - mu) * rstd * g_ref[:, cols] + b_ref[:, cols]
        o_ref[:, cols] = DEEPNORM_ALPHA * h + acc[:, cols]


def _matmul_resid(lhs, w, resid, ln=None, *, tm, tn, name):
    m = lhs[0][0].shape[0]
    n = w.shape[1]
    tm, tn = min(tm, m), min(tn, n)
    in_specs = [pl.BlockSpec((tm, width), functools.partial(lambda i, j, c: (i, c), c=col),
                             pipeline_mode=pl.Buffered(buffers))
                for _, width, col, buffers in lhs]
    in_specs += [pl.BlockSpec((w.shape[0], tn), lambda i, j: (0, j)), pl.BlockSpec((tm, tn), lambda i, j: (i, j))]
    operands = [*(a for a, _, _, _ in lhs), w, resid]
    if ln is not None:
        in_specs += [pl.BlockSpec((tm, 2 * V7X_LANES), lambda i, j: (i, 0)),
                     pl.BlockSpec((1, tn), lambda i, j: (0, j)), pl.BlockSpec((1, tn), lambda i, j: (0, j))]
        operands += list(ln)
    return pl.pallas_call(
        functools.partial(_mm_resid_kernel, n_lhs=len(lhs), ln_resid=ln is not None),
        grid=(m // tm, n // tn),
        in_specs=in_specs,
        out_specs=pl.BlockSpec((tm, tn), lambda i, j: (i, j)),
        out_shape=jax.ShapeDtypeStruct((m, n), _F32),
        compiler_params=_params("parallel", "parallel"),
        name=name,
    )(*operands)


def _xattn_kernel(h_ref, wq_ref, k_ref, v_ref, o_ref, *, scale):
    q = _dot(h_ref[...], wq_ref[...]).astype(_BF16)
    s = _dot_nt(q, k_ref[...]) * scale
    m = jnp.max(s, axis=-1, keepdims=True)
    p = jnp.exp(s - m)
    l = jnp.sum(p, axis=-1, keepdims=True)
    o = _dot(p.astype(_BF16), v_ref[...]) / l
    o_ref[...] = o.astype(o_ref.dtype)


def _cross_attention(h, wq, kv, *, seq, mem_len, tm, name):
    t, d = h.shape
    hd = d // XATTN_HEADS
    tm = min(tm, seq)
    seq_tiles = seq // tm
    return pl.pallas_call(
        functools.partial(_xattn_kernel, scale=hd ** -0.5),
        grid=(t // tm, XATTN_HEADS),
        in_specs=[
            pl.BlockSpec((tm, d), lambda i, hh: (i, 0)),
            pl.BlockSpec((d, hd), lambda i, hh: (0, hh)),
            pl.BlockSpec((mem_len, hd), lambda i, hh: (i // seq_tiles, hh)),
            pl.BlockSpec((mem_len, hd), lambda i, hh: (i // seq_tiles, XATTN_HEADS + hh)),
        ],
        out_specs=pl.BlockSpec((tm, hd), lambda i, hh: (i, hh)),
        out_shape=jax.ShapeDtypeStruct((t, d), _BF16),
        compiler_params=_params("parallel", "parallel"),
        name=name,
    )(h, wq, kv, kv)


def _gate_up_kernel(x_ref, wg_ref, wu_ref, o_ref):
    x = x_ref[...]
    g = _dot(x, wg_ref[...].astype(x.dtype))
    u = _dot(x, wu_ref[...].astype(x.dtype))
    o_ref[...] = (g * jax.nn.sigmoid(g) * u).astype(o_ref.dtype)


def _gate_up(x, wg, wu, *, tm, tn, name):
    m, k = x.shape
    n = wg.shape[1]
    tm, tn = min(tm, m), min(tn, n)
    return pl.pallas_call(
        _gate_up_kernel,
        grid=(m // tm, n // tn),
        in_specs=[
            pl.BlockSpec((tm, k), lambda i, j: (i, 0), pipeline_mode=pl.Buffered(1)),
            pl.BlockSpec((k, tn), lambda i, j: (0, j)),
            pl.BlockSpec((k, tn), lambda i, j: (0, j)),
        ],
        out_specs=pl.BlockSpec((tm, tn), lambda i, j: (i, j)),
        out_shape=jax.ShapeDtypeStruct((m, n), _BF16),
        compiler_params=_params("parallel", "parallel"),
        name=name,
    )(x, wg, wu)


def _rope_tables(seq):
    half = HEAD_DIM // 2
    inv = ROPE_THETA ** (-jnp.arange(half, dtype=_F32) / half)
    ang = jnp.arange(seq, dtype=jnp.int32).astype(_F32)[:, None] * inv[None, :]
    cos, sin = jnp.cos(ang), jnp.sin(ang)
    return jnp.concatenate([cos, cos], axis=-1), jnp.concatenate([-sin, sin], axis=-1)


def _layer(h, mem_b, w_in, w_pool, pool_scale, w_out, ln1_g, ln1_b, w_xq, w_xkv, w_xo, ln2_g, ln2_b,
           w_gate, w_up, w_down, ln3_g, ln3_b, *, batch, seq):
    pool_width = w_pool.shape[0] * w_pool.shape[1]
    attn_width = (w_in.shape[1] - pool_width) // 3
    mem_len = mem_b.shape[0] // batch
    assert w_gate.shape[1] % FFN_TILE == 0

    w_in_b = w_in.astype(_BF16)
    cos, sin = _rope_tables(seq)
    row = lambda v: v.reshape(1, -1)
    ln1, ln2, ln3 = (row(ln1_g), row(ln1_b)), (row(ln2_g), row(ln2_b)), (row(ln3_g), row(ln3_b))

    hp, hb = _matmul_cast(h, w_in_b, col0=0, n=pool_width, tm=512, tn=pool_width, out_dtype=_BF16,
                          name="in_proj_pool")
    qk = _matmul_rope(hb, w_in_b, cos, sin, col0=pool_width, n=2 * attn_width, seq=seq, q_width=attn_width,
                      q_scale=HEAD_DIM ** -0.5 * LOG2_E, tm=1024, tn=1024, name="in_proj_qk_rope")
    vt = _matmul_vt(hb, w_in_b, col0=pool_width + 2 * attn_width, n=attn_width, tm=1024, tn=1024,
                    name="in_proj_vt")
    pool_out = _pool_mixer(hp, w_pool.astype(_BF16), row(pool_scale), seq=seq, tm=512, name="pool_mixer")
    attn = _moba_attention(qk, vt, batch=batch, seq=seq, attn_width=attn_width, heads_per_step=4,
                           name="moba_attention")
    y1 = _matmul_resid([(pool_out, pool_width, 0, 2), (attn, attn_width, 0, 2)], w_out.astype(_BF16), h,
                       tm=1024, tn=512, name="mix_out")
    h1b, stats1 = _layer_norm(y1, *ln1, out_dtype=_BF16, with_stats=True, tm=256, name="ln1")

    kv = _matmul_wcast(mem_b, w_xkv, tm=512, tn=512, out_dtype=_BF16, name="xattn_kv")
    xo = _cross_attention(h1b, w_xq.astype(_BF16), kv, seq=seq, mem_len=mem_len, tm=1024, name="xattn_q_attend")
    y2 = _matmul_resid([(xo, xo.shape[1], 0, 2)], w_xo.astype(_BF16), y1, (stats1, *ln1), tm=1024, tn=512,
                       name="xattn_out")
    h2b, stats2 = _layer_norm(y2, *ln2, out_dtype=_BF16, with_stats=True, tm=256, name="ln2")

    act = _gate_up(h2b, w_gate, w_up, tm=2048, tn=FFN_TILE, name="ffn_gate_up")
    half = act.shape[1] // 2
    y3 = _matmul_resid([(act, half, 0, 2), (act, half, 1, 1)], w_down.astype(_BF16), y2, (stats2, *ln2),
                       tm=1024, tn=256, name="ffn_down")
    return _layer_norm(y3, *ln3, out_dtype=_F32, with_stats=False, tm=256, name="ln3")


def kernel(x, mem, w_mix_in, w_pool, pool_scale, w_mix_out, ln1_g, ln1_b, w_xq, w_xkv, w_xo, ln2_g, ln2_b,
           w_gate, w_up, w_down, ln3_g, ln3_b):
    batch, seq, d = x.shape
    assert w_mix_in.shape[0] == DEPTH
    h = x.reshape(batch * seq, d)
    mem_b = mem.reshape(-1, d).astype(_BF16)
    for l in range(DEPTH):
        h = _layer(h, mem_b, w_mix_in[l], w_pool[l], pool_scale[l], w_mix_out[l],
                   ln1_g[l], ln1_b[l], w_xq[l], w_xkv[l], w_xo[l], ln2_g[l], ln2_b[l],
                   w_gate[l], w_up[l], w_down[l], ln3_g[l], ln3_b[l], batch=batch, seq=seq)
    return h.reshape(batch, seq, d)
```

```python
import functools

import jax
import jax.numpy as jnp
from jax import lax
from jax.experimental import pallas as pl
from jax.experimental.pallas import tpu as pltpu

POOL_WINDOWS = (2, 4, 8, 16)
HEAD_DIM = 128
MOBA_BLOCK = 256
MOBA_TOPK = 3
ROPE_THETA = 10000.0
XATTN_HEADS = 4
LN_EPS = 1e-5
DEPTH = 1
DEEPNORM_ALPHA = (2 * DEPTH) ** 0.25

V7X_LANES = 128
V7X_VMEM_LIMIT_BYTES = 56 * 1024 * 1024
MASK_VALUE = -1e30
LOG2_E = 1.4426950408889634
VT_ROWS = HEAD_DIM + 16
POOL_HALO = 16
FFN_TILE = 256

_F32 = jnp.float32
_BF16 = jnp.bfloat16
_NT = (((1,), (1,)), ((), ()))


def _params(*semantics):
    return pltpu.CompilerParams(dimension_semantics=semantics, vmem_limit_bytes=V7X_VMEM_LIMIT_BYTES)


def _dot(a, b):
    return jnp.dot(a, b, preferred_element_type=_F32)


def _dot_nt(a, b):
    return lax.dot_general(a, b, _NT, preferred_element_type=_F32)


def _w_spec(k, tn, col0, n):
    off = col0 // tn
    if n == tn:
        return pl.BlockSpec((k, tn), lambda i, j: (0, off), pipeline_mode=pl.Buffered(1))
    return pl.BlockSpec((k, tn), lambda i, j: (0, j + off))


def _mm_wcast_kernel(x_ref, w_ref, o_ref):
    o_ref[...] = _dot(x_ref[...], w_ref[...].astype(x_ref.dtype)).astype(o_ref.dtype)


def _matmul_wcast(x, w, *, tm, tn, out_dtype, name):
    m, k = x.shape
    n = w.shape[1]
    tm, tn = min(tm, m), min(tn, n)
    return pl.pallas_call(
        _mm_wcast_kernel,
        grid=(m // tm, n // tn),
        in_specs=[pl.BlockSpec((tm, k), lambda i, j: (i, 0)), pl.BlockSpec((k, tn), lambda i, j: (0, j))],
        out_specs=pl.BlockSpec((tm, tn), lambda i, j: (i, j)),
        out_shape=jax.ShapeDtypeStruct((m, n), out_dtype),
        compiler_params=_params("parallel", "parallel"),
        name=name,
    )(x, w)


def _mm_cast_kernel(x_ref, w_ref, o_ref, xb_ref):
    @pl.when(pl.program_id(1) == 0)
    def _():
        xb_ref[...] = x_ref[...].astype(xb_ref.dtype)

    o_ref[...] = _dot(xb_ref[...], w_ref[...]).astype(o_ref.dtype)


def _matmul_cast(x, w, *, col0, n, tm, tn, out_dtype, name):
    m, k = x.shape
    tm, tn = min(tm, m), min(tn, n)
    return pl.pallas_call(
        _mm_cast_kernel,
        grid=(m // tm, n // tn),
        in_specs=[pl.BlockSpec((tm, k), lambda i, j: (i, 0)), _w_spec(k, tn, col0, n)],
        out_specs=[pl.BlockSpec((tm, tn), lambda i, j: (i, j)), pl.BlockSpec((tm, k), lambda i, j: (i, 0))],
        out_shape=[jax.ShapeDtypeStruct((m, n), out_dtype), jax.ShapeDtypeStruct((m, k), _BF16)],
        compiler_params=_params("parallel", "arbitrary"),
        name=name,
    )(x, w)


def _mm_rope_kernel(x_ref, w_ref, cos_ref, sin_ref, o_ref, *, scale):
    acc = _dot(x_ref[...], w_ref[...])
    cos = cos_ref[...] * scale
    sin = sin_ref[...] * scale
    for h in range(acc.shape[1] // HEAD_DIM):
        cols = slice(h * HEAD_DIM, (h + 1) * HEAD_DIM)
        t = acc[:, cols]
        r = (t * cos + pltpu.roll(t, HEAD_DIM // 2, 1) * sin).astype(o_ref.dtype)
        if len(o_ref.shape) == 2:
            o_ref[:, cols] = r
        else:
            for kb in range(o_ref.shape[1]):
                o_ref[h, kb] = r[kb * MOBA_BLOCK:(kb + 1) * MOBA_BLOCK, :]


def _matmul_rope(x, w, cos, sin, *, col0, n, seq, scale, blocked, tm, tn, name):
    m, k = x.shape
    tm, tn = min(tm, seq), min(tn, n)
    seq_tiles = seq // tm
    if blocked:
        nh, nkb = tn // HEAD_DIM, tm // MOBA_BLOCK
        out_spec = pl.BlockSpec((nh, nkb, MOBA_BLOCK, HEAD_DIM), lambda i, j: (j, i, 0, 0))
        out_shape = jax.ShapeDtypeStruct((n // HEAD_DIM, m // MOBA_BLOCK, MOBA_BLOCK, HEAD_DIM), _BF16)
    else:
        out_spec = pl.BlockSpec((tm, tn), lambda i, j: (i, j))
        out_shape = jax.ShapeDtypeStruct((m, n), _BF16)
    return pl.pallas_call(
        functools.partial(_mm_rope_kernel, scale=scale),
        grid=(m // tm, n // tn),
        in_specs=[
            pl.BlockSpec((tm, k), lambda i, j: (i, 0)),
            _w_spec(k, tn, col0, n),
            pl.BlockSpec((tm, HEAD_DIM), lambda i, j: (i % seq_tiles, 0)),
            pl.BlockSpec((tm, HEAD_DIM), lambda i, j: (i % seq_tiles, 0)),
        ],
        out_specs=out_spec,
        out_shape=out_shape,
        compiler_params=_params("parallel", "parallel"),
        name=name,
    )(x, w, cos, sin)


def _mm_vt_kernel(x_ref, w_ref, o_ref):
    acc = _dot(x_ref[...], w_ref[...])
    ones = jnp.ones((VT_ROWS - HEAD_DIM, MOBA_BLOCK), o_ref.dtype)
    for h in range(o_ref.shape[0]):
        for kb in range(o_ref.shape[1]):
            v = acc[kb * MOBA_BLOCK:(kb + 1) * MOBA_BLOCK, h * HEAD_DIM:(h + 1) * HEAD_DIM]
            o_ref[h, kb, :HEAD_DIM, :] = v.T.astype(o_ref.dtype)
            o_ref[h, kb, HEAD_DIM:, :] = ones


def _matmul_vt(x, w, *, col0, n, tm, tn, name):
    m, k = x.shape
    tm, tn = min(tm, m), min(tn, n)
    nh, nkb = tn // HEAD_DIM, tm // MOBA_BLOCK
    return pl.pallas_call(
        _mm_vt_kernel,
        grid=(m // tm, n // tn),
        in_specs=[pl.BlockSpec((tm, k), lambda i, j: (i, 0)), _w_spec(k, tn, col0, n)],
        out_specs=pl.BlockSpec((nh, nkb, VT_ROWS, MOBA_BLOCK), lambda i, j: (j, i, 0, 0)),
        out_shape=jax.ShapeDtypeStruct((n // HEAD_DIM, m // MOBA_BLOCK, VT_ROWS, MOBA_BLOCK), _BF16),
        compiler_params=_params("parallel", "parallel"),
        name=name,
    )(x, w)


def _pool_kernel(halo_ref, u_ref, w_ref, scale_ref, o_ref, ext_ref, *, seq_tiles):
    tm = u_ref.shape[0]
    gw = w_ref.shape[1]
    first = pl.program_id(0) % seq_tiles == 0
    halo = halo_ref[...].astype(_F32)
    ext_ref[0:POOL_HALO, :] = jnp.where(first, 0.0, halo)
    ext_ref[POOL_HALO:, :] = u_ref[...].astype(_F32)
    pos = (pl.program_id(0) % seq_tiles) * tm + lax.broadcasted_iota(jnp.int32, (tm, gw), 0)
    for g, win in enumerate(POOL_WINDOWS):
        cols = slice(g * gw, (g + 1) * gw)
        u = ext_ref[POOL_HALO:, cols]
        s = u
        for d in range(1, win):
            s = s + ext_ref[POOL_HALO - d:POOL_HALO - d + tm, cols]
        cnt = jnp.minimum(pos + 1, win).astype(_F32)
        mixed = (s / cnt - u).astype(_BF16)
        y = _dot(mixed, w_ref[g]) * scale_ref[:, cols]
        o_ref[:, cols] = y.astype(o_ref.dtype)


def _pool_mixer(hp, w_pool, scale, *, seq, tm, name):
    m, width = hp.shape
    tm = min(tm, seq)
    seq_tiles = seq // tm
    halo_per_tile = tm // POOL_HALO
    return pl.pallas_call(
        functools.partial(_pool_kernel, seq_tiles=seq_tiles),
        grid=(m // tm,),
        in_specs=[
            pl.BlockSpec((POOL_HALO, width), lambda i: (jnp.maximum(i * halo_per_tile - 1, 0), 0)),
            pl.BlockSpec((tm, width), lambda i: (i, 0)),
            pl.BlockSpec(w_pool.shape, lambda i: (0, 0, 0)),
            pl.BlockSpec((1, width), lambda i: (0, 0)),
        ],
        out_specs=pl.BlockSpec((tm, width), lambda i: (i, 0)),
        out_shape=jax.ShapeDtypeStruct((m, width), _BF16),
        scratch_shapes=[pltpu.VMEM((tm + POOL_HALO, width), _F32)],
        compiler_params=_params("parallel"),
        name=name,
    )(hp, hp, w_pool, scale)


def _moba_kernel(q_ref, k_ref, vt_ref, o_ref, kmean_ref, qt_ref, bias_ref, m_ref, alpha_ref, acc_ref, p_ref,
                 s_ref, *, heads):
    n = pl.program_id(2)
    nb = kmean_ref.shape[1]
    bq = q_ref.shape[0]
    head_cols = [slice(h * HEAD_DIM, (h + 1) * HEAD_DIM) for h in range(heads)]

    @pl.when(n == 0)
    def _():
        def block_mean(j, carry):
            for h in range(heads):
                kb = k_ref[h, j].astype(_F32)
                kmean_ref[h, pl.ds(j, 1), :] = jnp.mean(kb, axis=0, keepdims=True)
            return carry
        lax.fori_loop(0, nb, block_mean, 0)

    def scores(block, parity, bias_row=None, keep=None):
        for h in range(heads):
            s = _dot(k_ref[h, block], qt_ref[h])
            if keep is not None:
                s_ref[parity, h] = jnp.where(keep, s, MASK_VALUE)
            else:
                s_ref[parity, h] = s + bias_ref[h, pl.ds(bias_row, 1), :]

    def weigh_values(block, parity):
        for h in range(heads):
            acc_ref[h] = alpha_ref[parity, h] * acc_ref[h] + _dot(vt_ref[h, block], p_ref[parity, h])

    def softmax_update(parity):
        for h in range(heads):
            m_old = m_ref[h]
            m_new = jnp.maximum(m_old, jnp.max(s_ref[parity, h], axis=0, keepdims=True))
            m_ref[h] = m_new
            alpha_ref[parity, h] = jnp.exp2(m_old - m_new)
            p_ref[parity, h] = jnp.exp2(s_ref[parity, h] - m_new).astype(p_ref.dtype)

    def step(item, parity):
        weigh_values(jnp.where(item == 1, n, jnp.maximum(item - 2, 0)), 1 - parity)
        scores(jnp.minimum(item, jnp.maximum(n - 1, 0)), 1 - parity, bias_row=item)
        softmax_update(parity)

    blk = lax.broadcasted_iota(jnp.int32, (nb, bq), 0).astype(_F32)
    n_f = n.astype(_F32)
    for h in range(heads):
        qt = q_ref[:, head_cols[h]].astype(_F32).T.astype(qt_ref.dtype)
        qt_ref[h] = qt
        km = kmean_ref[h]
        km_hi = km.astype(_BF16)
        km_lo = (km - km_hi.astype(_F32)).astype(_BF16)
        g = jnp.where(blk < n_f, _dot(km_hi, qt) + _dot(km_lo, qt), -jnp.inf)
        bias = jnp.full((nb, bq), MASK_VALUE, _F32)
        for _ in range(MOBA_TOPK):
            top = jnp.max(g, axis=0, keepdims=True)
            is_top = (g == top) & (g > -jnp.inf)
            first = jnp.min(jnp.where(is_top, blk, float(nb)), axis=0, keepdims=True)
            pick = blk == first
            bias = jnp.where(pick, 0.0, bias)
            g = jnp.where(pick, -jnp.inf, g)
        bias_ref[h] = bias
        m_ref[h] = jnp.full((1, bq), MASK_VALUE, _F32)
        alpha_ref[1, h] = jnp.zeros((1, bq), _F32)
        acc_ref[h] = jnp.zeros(acc_ref.shape[1:], _F32)
        p_ref[1, h] = jnp.zeros(p_ref.shape[2:], p_ref.dtype)

    kidx = lax.broadcasted_iota(jnp.int32, (MOBA_BLOCK, bq), 0)
    qidx = lax.broadcasted_iota(jnp.int32, (MOBA_BLOCK, bq), 1)
    scores(n, 0, keep=kidx <= qidx)

    def two_steps(t, carry):
        step(2 * t, 0)
        step(2 * t + 1, 1)
        return carry
    lax.fori_loop(0, lax.shift_right_logical(n + 1, 1), two_steps, 0)

    last_block = jnp.maximum(n - 1, 0)

    @pl.when(lax.bitwise_and(n, 1) == 0)
    def _():
        step(n, 0)
        weigh_values(last_block, 0)

    @pl.when(lax.bitwise_and(n, 1) == 1)
    def _():
        weigh_values(last_block, 1)

    for h in range(heads):
        out_t = acc_ref[h, :HEAD_DIM, :] / acc_ref[h, HEAD_DIM:HEAD_DIM + 1, :]
        o_ref[:, head_cols[h]] = out_t.T.astype(o_ref.dtype)


def _moba_attention(q, k, vt, *, batch, seq, attn_width, heads_per_step, name):
    t = q.shape[0]
    nb = seq // MOBA_BLOCK
    gw = heads_per_step * HEAD_DIM
    return pl.pallas_call(
        functools.partial(_moba_kernel, heads=heads_per_step),
        grid=(batch, attn_width // gw, nb),
        in_specs=[
            pl.BlockSpec((MOBA_BLOCK, gw), lambda b, g, n: (b * nb + n, g)),
            pl.BlockSpec((heads_per_step, nb, MOBA_BLOCK, HEAD_DIM), lambda b, g, n: (g, b, 0, 0)),
            pl.BlockSpec((heads_per_step, nb, VT_ROWS, MOBA_BLOCK), lambda b, g, n: (g, b, 0, 0)),
        ],
        out_specs=pl.BlockSpec((MOBA_BLOCK, gw), lambda b, g, n: (b * nb + n, g)),
        out_shape=jax.ShapeDtypeStruct((t, attn_width), _BF16),
        scratch_shapes=[
            pltpu.VMEM((heads_per_step, nb, HEAD_DIM), _F32),
            pltpu.VMEM((heads_per_step, HEAD_DIM, MOBA_BLOCK), _BF16),
            pltpu.VMEM((heads_per_step, nb, MOBA_BLOCK), _F32),
            pltpu.VMEM((heads_per_step, 1, MOBA_BLOCK), _F32),
            pltpu.VMEM((2, heads_per_step, 1, MOBA_BLOCK), _F32),
            pltpu.VMEM((heads_per_step, VT_ROWS, MOBA_BLOCK), _F32),
            pltpu.VMEM((2, heads_per_step, MOBA_BLOCK, MOBA_BLOCK), _BF16),
            pltpu.VMEM((2, heads_per_step, MOBA_BLOCK, MOBA_BLOCK), _F32),
        ],
        compiler_params=_params("parallel", "parallel", "arbitrary"),
        name=name,
    )(q, k, vt)


def _ln_kernel(y_ref, g_ref, b_ref, o_ref, *maybe_stats_ref):
    y = y_ref[...]
    mu = jnp.mean(y, axis=-1, keepdims=True)
    yc = y - mu
    var = jnp.mean(yc * yc, axis=-1, keepdims=True)
    rstd = lax.rsqrt(var + LN_EPS)
    o_ref[...] = (yc * rstd * g_ref[...] + b_ref[...]).astype(o_ref.dtype)
    for stats_ref in maybe_stats_ref:
        stats_ref[:, :V7X_LANES] = jnp.broadcast_to(mu, (y.shape[0], V7X_LANES))
        stats_ref[:, V7X_LANES:] = jnp.broadcast_to(rstd, (y.shape[0], V7X_LANES))


def _layer_norm(y, gain, shift, *, out_dtype, with_stats, tm, name):
    m, n = y.shape
    tm = min(tm, m)
    out_shape = [jax.ShapeDtypeStruct((m, n), out_dtype)]
    out_specs = [pl.BlockSpec((tm, n), lambda i: (i, 0))]
    if with_stats:
        out_shape.append(jax.ShapeDtypeStruct((m, 2 * V7X_LANES), _F32))
        out_specs.append(pl.BlockSpec((tm, 2 * V7X_LANES), lambda i: (i, 0)))
    outs = pl.pallas_call(
        _ln_kernel,
        grid=(m // tm,),
        in_specs=[
            pl.BlockSpec((tm, n), lambda i: (i, 0)),
            pl.BlockSpec((1, n), lambda i: (0, 0)),
            pl.BlockSpec((1, n), lambda i: (0, 0)),
        ],
        out_specs=out_specs,
        out_shape=out_shape,
        compiler_params=_params("parallel"),
        name=name,
    )(y, gain, shift)
    return outs if with_stats else outs[0]


def _mm_resid_kernel(*refs, n_lhs, ln_resid):
    lhs_refs, w_ref, r_ref, o_ref = refs[:n_lhs], refs[n_lhs], refs[n_lhs + 1], refs[-1]
    acc, k0 = None, 0
    for a_ref in lhs_refs:
        part = _dot(a_ref[...], w_ref[k0:k0 + a_ref.shape[1], :])
        acc = part if acc is None else acc + part
        k0 += a_ref.shape[1]
    if not ln_resid:
        o_ref[...] = DEEPNORM_ALPHA * r_ref[...] + acc
        return
    stats_ref, g_ref, b_ref = refs[n_lhs + 2:n_lhs + 5]
    mu = stats_ref[:, :V7X_LANES]
    rstd = stats_ref[:, V7X_LANES:]
    for c in range(o_ref.shape[1] // V7X_LANES):
        cols = slice(c * V7X_LANES, (c + 1) * V7X_LANES)
        h = (r_ref[:, cols] - mu) * rstd * g_ref[:, cols] + b_ref[:, cols]
        o_ref[:, cols] = DEEPNORM_ALPHA * h + acc[:, cols]


def _matmul_resid(lhs, w, resid, ln=None, *, tm, tn, name):
    m = lhs[0][0].shape[0]
    n = w.shape[1]
    tm, tn = min(tm, m), min(tn, n)
    in_specs = [pl.BlockSpec((tm, width), functools.partial(lambda i, j, c: (i, c), c=col),
                             pipeline_mode=pl.Buffered(buffers))
                for _, width, col, buffers in lhs]
    in_specs += [pl.BlockSpec((w.shape[0], tn), lambda i, j: (0, j)), pl.BlockSpec((tm, tn), lambda i, j: (i, j))]
    operands = [*(a for a, _, _, _ in lhs), w, resid]
    if ln is not None:
        in_specs += [pl.BlockSpec((tm, 2 * V7X_LANES), lambda i, j: (i, 0)),
                     pl.BlockSpec((1, tn), lambda i, j: (0, j)), pl.BlockSpec((1, tn), lambda i, j: (0, j))]
        operands += list(ln)
    return pl.pallas_call(
        functools.partial(_mm_resid_kernel, n_lhs=len(lhs), ln_resid=ln is not None),
        grid=(m // tm, n // tn),
        in_specs=in_specs,
        out_specs=pl.BlockSpec((tm, tn), lambda i, j: (i, j)),
        out_shape=jax.ShapeDtypeStruct((m, n), _F32),
        compiler_params=_params("parallel", "parallel"),
        name=name,
    )(*operands)


def _xattn_kernel(h_ref, wq_ref, k_ref, v_ref, o_ref, *, scale):
    q = _dot(h_ref[...], wq_ref[...]).astype(_BF16)
    s = _dot_nt(q, k_ref[...]) * scale
    m = jnp.max(s, axis=-1, keepdims=True)
    p = jnp.exp(s - m)
    l = jnp.sum(p, axis=-1, keepdims=True)
    o = _dot(p.astype(_BF16), v_ref[...]) / l
    o_ref[...] = o.astype(o_ref.dtype)


def _cross_attention(h, wq, kv, *, seq, mem_len, tm, name):
    t, d = h.shape
    hd = d // XATTN_HEADS
    tm = min(tm, seq)
    seq_tiles = seq // tm
    return pl.pallas_call(
        functools.partial(_xattn_kernel, scale=hd ** -0.5),
        grid=(t // tm, XATTN_HEADS),
        in_specs=[
            pl.BlockSpec((tm, d), lambda i, hh: (i, 0)),
            pl.BlockSpec((d, hd), lambda i, hh: (0, hh)),
            pl.BlockSpec((mem_len, hd), lambda i, hh: (i // seq_tiles, hh)),
            pl.BlockSpec((mem_len, hd), lambda i, hh: (i // seq_tiles, XATTN_HEADS + hh)),
        ],
        out_specs=pl.BlockSpec((tm, hd), lambda i, hh: (i, hh)),
        out_shape=jax.ShapeDtypeStruct((t, d), _BF16),
        compiler_params=_params("parallel", "parallel"),
        name=name,
    )(h, wq, kv, kv)


def _gate_up_kernel(x_ref, wg_ref, wu_ref, o_ref):
    x = x_ref[...]
    g = _dot(x, wg_ref[...].astype(x.dtype))
    u = _dot(x, wu_ref[...].astype(x.dtype))
    o_ref[...] = (g * jax.nn.sigmoid(g) * u).astype(o_ref.dtype)


def _gate_up(x, wg, wu, *, tm, tn, name):
    m, k = x.shape
    n = wg.shape[1]
    tm, tn = min(tm, m), min(tn, n)
    return pl.pallas_call(
        _gate_up_kernel,
        grid=(m // tm, n // tn),
        in_specs=[
            pl.BlockSpec((tm, k), lambda i, j: (i, 0), pipeline_mode=pl.Buffered(1)),
            pl.BlockSpec((k, tn), lambda i, j: (0, j)),
            pl.BlockSpec((k, tn), lambda i, j: (0, j)),
        ],
        out_specs=pl.BlockSpec((tm, tn), lambda i, j: (i, j)),
        out_shape=jax.ShapeDtypeStruct((m, n), _BF16),
        compiler_params=_params("parallel", "parallel"),
        name=name,
    )(x, wg, wu)


def _rope_tables(seq):
    half = HEAD_DIM // 2
    inv = ROPE_THETA ** (-jnp.arange(half, dtype=_F32) / half)
    ang = jnp.arange(seq, dtype=jnp.int32).astype(_F32)[:, None] * inv[None, :]
    cos, sin = jnp.cos(ang), jnp.sin(ang)
    return jnp.concatenate([cos, cos], axis=-1), jnp.concatenate([-sin, sin], axis=-1)


def _layer(h, mem_b, w_in, w_pool, pool_scale, w_out, ln1_g, ln1_b, w_xq, w_xkv, w_xo, ln2_g, ln2_b,
           w_gate, w_up, w_down, ln3_g, ln3_b, *, batch, seq):
    pool_width = w_pool.shape[0] * w_pool.shape[1]
    attn_width = (w_in.shape[1] - pool_width) // 3
    mem_len = mem_b.shape[0] // batch
    assert w_gate.shape[1] % FFN_TILE == 0

    w_in_b = w_in.astype(_BF16)
    cos, sin = _rope_tables(seq)
    row = lambda v: v.reshape(1, -1)
    ln1, ln2, ln3 = (row(ln1_g), row(ln1_b)), (row(ln2_g), row(ln2_b)), (row(ln3_g), row(ln3_b))

    hp, hb = _matmul_cast(h, w_in_b, col0=0, n=pool_width, tm=512, tn=pool_width, out_dtype=_BF16,
                          name="in_proj_pool")
    q = _matmul_rope(hb, w_in_b, cos, sin, col0=pool_width, n=attn_width, seq=seq, scale=HEAD_DIM ** -0.5 * LOG2_E,
                     blocked=False, tm=1024, tn=1024, name="in_proj_q_rope")
    k = _matmul_rope(hb, w_in_b, cos, sin, col0=pool_width + attn_width, n=attn_width, seq=seq, scale=1.0,
                     blocked=True, tm=1024, tn=1024, name="in_proj_k_rope")
    vt = _matmul_vt(hb, w_in_b, col0=pool_width + 2 * attn_width, n=attn_width, tm=1024, tn=1024,
                    name="in_proj_vt")
    pool_out = _pool_mixer(hp, w_pool.astype(_BF16), row(pool_scale), seq=seq, tm=512, name="pool_mixer")
    attn = _moba_attention(q, k, vt, batch=batch, seq=seq, attn_width=attn_width, heads_per_step=4,
                           name="moba_attention")
    y1 = _matmul_resid([(pool_out, pool_width, 0, 2), (attn, attn_width, 0, 2)], w_out.astype(_BF16), h,
                       tm=1024, tn=512, name="mix_out")
    h1b, stats1 = _layer_norm(y1, *ln1, out_dtype=_BF16, with_stats=True, tm=256, name="ln1")

    kv = _matmul_wcast(mem_b, w_xkv, tm=512, tn=512, out_dtype=_BF16, name="xattn_kv")
    xo = _cross_attention(h1b, w_xq.astype(_BF16), kv, seq=seq, mem_len=mem_len, tm=1024, name="xattn_q_attend")
    y2 = _matmul_resid([(xo, xo.shape[1], 0, 2)], w_xo.astype(_BF16), y1, (stats1, *ln1), tm=1024, tn=512,
                       name="xattn_out")
    h2b, stats2 = _layer_norm(y2, *ln2, out_dtype=_BF16, with_stats=True, tm=256, name="ln2")

    act = _gate_up(h2b, w_gate, w_up, tm=2048, tn=FFN_TILE, name="ffn_gate_up")
    y3 = _matmul_resid([(act, act.shape[1], 0, 1)], w_down.astype(_BF16), y2, (stats2, *ln2),
                       tm=1024, tn=256, name="ffn_down")
    return _layer_norm(y3, *ln3, out_dtype=_F32, with_stats=False, tm=256, name="ln3")


def kernel(x, mem, w_mix_in, w_pool, pool_scale, w_mix_out, ln1_g, ln1_b, w_xq, w_xkv, w_xo, ln2_g, ln2_b,
           w_gate, w_up, w_down, ln3_g, ln3_b):
    batch, seq, d = x.shape
    assert w_mix_in.shape[0] == DEPTH
    h = x.reshape(batch * seq, d)
    mem_b = mem.reshape(-1, d).astype(_BF16)
    for l in range(DEPTH):
        h = _layer(h, mem_b, w_mix_in[l], w_pool[l], pool_scale[l], w_mix_out[l],
                   ln1_g[l], ln1_b[l], w_xq[l], w_xkv[l], w_xo[l], ln2_g[l], ln2_b[l],
                   w_gate[l], w_up[l], w_down[l], ln3_g[l], ln3_b[l], batch=batch, seq=seq)
    return h.reshape(batch, seq, d)
```

```python
import functools

import jax
import jax.numpy as jnp
from jax import lax
from jax.experimental import pallas as pl
from jax.experimental.pallas import tpu as pltpu

POOL_WINDOWS = (2, 4, 8, 16)
HEAD_DIM = 128
MOBA_BLOCK = 256
MOBA_TOPK = 3
ROPE_THETA = 10000.0
XATTN_HEADS = 4
LN_EPS = 1e-5
DEPTH = 1
DEEPNORM_ALPHA = (2 * DEPTH) ** 0.25

V7X_LANES = 128
V7X_VMEM_LIMIT_BYTES = 56 * 1024 * 1024
MASK_VALUE = -1e30
LOG2_E = 1.4426950408889634
VT_ROWS = HEAD_DIM + 16
POOL_HALO = 16
FFN_TILE = 256

_F32 = jnp.float32
_BF16 = jnp.bfloat16
_NT = (((1,), (1,)), ((), ()))


def _params(*semantics):
    return pltpu.CompilerParams(dimension_semantics=semantics, vmem_limit_bytes=V7X_VMEM_LIMIT_BYTES)


def _dot(a, b):
    return jnp.dot(a, b, preferred_element_type=_F32)


def _dot_nt(a, b):
    return lax.dot_general(a, b, _NT, preferred_element_type=_F32)


def _w_spec(k, tn, col0, n):
    off = col0 // tn
    if n == tn:
        return pl.BlockSpec((k, tn), lambda i, j: (0, off), pipeline_mode=pl.Buffered(1))
    return pl.BlockSpec((k, tn), lambda i, j: (0, j + off))


def _mm_wcast_kernel(x_ref, w_ref, o_ref):
    o_ref[...] = _dot(x_ref[...], w_ref[...].astype(x_ref.dtype)).astype(o_ref.dtype)


def _matmul_wcast(x, w, *, tm, tn, out_dtype, name):
    m, k = x.shape
    n = w.shape[1]
    tm, tn = min(tm, m), min(tn, n)
    return pl.pallas_call(
        _mm_wcast_kernel,
        grid=(m // tm, n // tn),
        in_specs=[pl.BlockSpec((tm, k), lambda i, j: (i, 0)), pl.BlockSpec((k, tn), lambda i, j: (0, j))],
        out_specs=pl.BlockSpec((tm, tn), lambda i, j: (i, j)),
        out_shape=jax.ShapeDtypeStruct((m, n), out_dtype),
        compiler_params=_params("parallel", "parallel"),
        name=name,
    )(x, w)


def _mm_cast_kernel(x_ref, w_ref, o_ref, xb_ref):
    @pl.when(pl.program_id(1) == 0)
    def _():
        xb_ref[...] = x_ref[...].astype(xb_ref.dtype)

    o_ref[...] = _dot(xb_ref[...], w_ref[...]).astype(o_ref.dtype)


def _matmul_cast(x, w, *, col0, n, tm, tn, out_dtype, name):
    m, k = x.shape
    tm, tn = min(tm, m), min(tn, n)
    return pl.pallas_call(
        _mm_cast_kernel,
        grid=(m // tm, n // tn),
        in_specs=[pl.BlockSpec((tm, k), lambda i, j: (i, 0)), _w_spec(k, tn, col0, n)],
        out_specs=[pl.BlockSpec((tm, tn), lambda i, j: (i, j)), pl.BlockSpec((tm, k), lambda i, j: (i, 0))],
        out_shape=[jax.ShapeDtypeStruct((m, n), out_dtype), jax.ShapeDtypeStruct((m, k), _BF16)],
        compiler_params=_params("parallel", "arbitrary"),
        name=name,
    )(x, w)


def _mm_rope_kernel(x_ref, w_ref, cos_ref, sin_ref, o_ref, *, scale):
    acc = _dot(x_ref[...], w_ref[...])
    cos = cos_ref[...] * scale
    sin = sin_ref[...] * scale
    for h in range(acc.shape[1] // HEAD_DIM):
        cols = slice(h * HEAD_DIM, (h + 1) * HEAD_DIM)
        t = acc[:, cols]
        r = (t * cos + pltpu.roll(t, HEAD_DIM // 2, 1) * sin).astype(o_ref.dtype)
        if len(o_ref.shape) == 2:
            o_ref[:, cols] = r
        else:
            for kb in range(o_ref.shape[1]):
                o_ref[h, kb] = r[kb * MOBA_BLOCK:(kb + 1) * MOBA_BLOCK, :]


def _matmul_rope(x, w, cos, sin, *, col0, n, seq, scale, blocked, tm, tn, name):
    m, k = x.shape
    tm, tn = min(tm, seq), min(tn, n)
    seq_tiles = seq // tm
    if blocked:
        nh, nkb = tn // HEAD_DIM, tm // MOBA_BLOCK
        out_spec = pl.BlockSpec((nh, nkb, MOBA_BLOCK, HEAD_DIM), lambda i, j: (j, i, 0, 0))
        out_shape = jax.ShapeDtypeStruct((n // HEAD_DIM, m // MOBA_BLOCK, MOBA_BLOCK, HEAD_DIM), _BF16)
    else:
        out_spec = pl.BlockSpec((tm, tn), lambda i, j: (i, j))
        out_shape = jax.ShapeDtypeStruct((m, n), _BF16)
    return pl.pallas_call(
        functools.partial(_mm_rope_kernel, scale=scale),
        grid=(m // tm, n // tn),
        in_specs=[
            pl.BlockSpec((tm, k), lambda i, j: (i, 0)),
            _w_spec(k, tn, col0, n),
            pl.BlockSpec((tm, HEAD_DIM), lambda i, j: (i % seq_tiles, 0)),
            pl.BlockSpec((tm, HEAD_DIM), lambda i, j: (i % seq_tiles, 0)),
        ],
        out_specs=out_spec,
        out_shape=out_shape,
        compiler_params=_params("parallel", "parallel"),
        name=name,
    )(x, w, cos, sin)


def _mm_vt_kernel(x_ref, w_ref, o_ref):
    acc = _dot(x_ref[...], w_ref[...])
    ones = jnp.ones((VT_ROWS - HEAD_DIM, MOBA_BLOCK), o_ref.dtype)
    for h in range(o_ref.shape[0]):
        for kb in range(o_ref.shape[1]):
            v = acc[kb * MOBA_BLOCK:(kb + 1) * MOBA_BLOCK, h * HEAD_DIM:(h + 1) * HEAD_DIM]
            o_ref[h, kb, :HEAD_DIM, :] = v.T.astype(o_ref.dtype)
            o_ref[h, kb, HEAD_DIM:, :] = ones


def _matmul_vt(x, w, *, col0, n, tm, tn, name):
    m, k = x.shape
    tm, tn = min(tm, m), min(tn, n)
    nh, nkb = tn // HEAD_DIM, tm // MOBA_BLOCK
    return pl.pallas_call(
        _mm_vt_kernel,
        grid=(m // tm, n // tn),
        in_specs=[pl.BlockSpec((tm, k), lambda i, j: (i, 0)), _w_spec(k, tn, col0, n)],
        out_specs=pl.BlockSpec((nh, nkb, VT_ROWS, MOBA_BLOCK), lambda i, j: (j, i, 0, 0)),
        out_shape=jax.ShapeDtypeStruct((n // HEAD_DIM, m // MOBA_BLOCK, VT_ROWS, MOBA_BLOCK), _BF16),
        compiler_params=_params("parallel", "parallel"),
        name=name,
    )(x, w)


def _pool_kernel(halo_ref, u_ref, w_ref, scale_ref, o_ref, ext_ref, *, seq_tiles):
    tm = u_ref.shape[0]
    gw = w_ref.shape[1]
    first = pl.program_id(0) % seq_tiles == 0
    halo = halo_ref[...].astype(_F32)
    ext_ref[0:POOL_HALO, :] = jnp.where(first, 0.0, halo)
    ext_ref[POOL_HALO:, :] = u_ref[...].astype(_F32)
    pos = (pl.program_id(0) % seq_tiles) * tm + lax.broadcasted_iota(jnp.int32, (tm, gw), 0)
    for g, win in enumerate(POOL_WINDOWS):
        cols = slice(g * gw, (g + 1) * gw)
        u = ext_ref[POOL_HALO:, cols]
        s = u
        for d in range(1, win):
            s = s + ext_ref[POOL_HALO - d:POOL_HALO - d + tm, cols]
        cnt = jnp.minimum(pos + 1, win).astype(_F32)
        mixed = (s / cnt - u).astype(_BF16)
        y = _dot(mixed, w_ref[g]) * scale_ref[:, cols]
        o_ref[:, cols] = y.astype(o_ref.dtype)


def _pool_mixer(hp, w_pool, scale, *, seq, tm, name):
    m, width = hp.shape
    tm = min(tm, seq)
    seq_tiles = seq // tm
    halo_per_tile = tm // POOL_HALO
    return pl.pallas_call(
        functools.partial(_pool_kernel, seq_tiles=seq_tiles),
        grid=(m // tm,),
        in_specs=[
            pl.BlockSpec((POOL_HALO, width), lambda i: (jnp.maximum(i * halo_per_tile - 1, 0), 0)),
            pl.BlockSpec((tm, width), lambda i: (i, 0)),
            pl.BlockSpec(w_pool.shape, lambda i: (0, 0, 0)),
            pl.BlockSpec((1, width), lambda i: (0, 0)),
        ],
        out_specs=pl.BlockSpec((tm, width), lambda i: (i, 0)),
        out_shape=jax.ShapeDtypeStruct((m, width), _BF16),
        scratch_shapes=[pltpu.VMEM((tm + POOL_HALO, width), _F32)],
        compiler_params=_params("parallel"),
        name=name,
    )(hp, hp, w_pool, scale)


def _moba_kernel(q_ref, k_ref, vt_ref, o_ref, kmean_ref, qt_ref, bias_ref, m_ref, alpha_ref, acc_ref, p_ref,
                 s_ref, *, heads):
    n = pl.program_id(2)
    nb = kmean_ref.shape[1]
    bq = q_ref.shape[0]
    head_cols = [slice(h * HEAD_DIM, (h + 1) * HEAD_DIM) for h in range(heads)]

    @pl.when(n == 0)
    def _():
        def block_mean(j, carry):
            for h in range(heads):
                kb = k_ref[h, j].astype(_F32)
                kmean_ref[h, pl.ds(j, 1), :] = jnp.mean(kb, axis=0, keepdims=True)
            return carry
        lax.fori_loop(0, nb, block_mean, 0)

    def scores(block, parity, bias_row=None, keep=None):
        for h in range(heads):
            s = _dot(k_ref[h, block], qt_ref[h])
            if keep is not None:
                s_ref[parity, h] = jnp.where(keep, s, MASK_VALUE)
            else:
                s_ref[parity, h] = s + bias_ref[h, pl.ds(bias_row, 1), :]

    def weigh_values(block, parity):
        for h in range(heads):
            acc_ref[h] = alpha_ref[parity, h] * acc_ref[h] + _dot(vt_ref[h, block], p_ref[parity, h])

    def softmax_update(parity):
        for h in range(heads):
            m_old = m_ref[h]
            m_new = jnp.maximum(m_old, jnp.max(s_ref[parity, h], axis=0, keepdims=True))
            m_ref[h] = m_new
            alpha_ref[parity, h] = jnp.exp2(m_old - m_new)
            p_ref[parity, h] = jnp.exp2(s_ref[parity, h] - m_new).astype(p_ref.dtype)

    def step(item, parity):
        weigh_values(jnp.where(item == 1, n, jnp.maximum(item - 2, 0)), 1 - parity)
        scores(jnp.minimum(item, jnp.maximum(n - 1, 0)), 1 - parity, bias_row=item)
        softmax_update(parity)

    blk = lax.broadcasted_iota(jnp.int32, (nb, bq), 0).astype(_F32)
    n_f = n.astype(_F32)
    for h in range(heads):
        qt = q_ref[:, head_cols[h]].astype(_F32).T.astype(qt_ref.dtype)
        qt_ref[h] = qt
        km = kmean_ref[h]
        km_hi = km.astype(_BF16)
        km_lo = (km - km_hi.astype(_F32)).astype(_BF16)
        g = jnp.where(blk < n_f, _dot(km_hi, qt) + _dot(km_lo, qt), -jnp.inf)
        bias = jnp.full((nb, bq), MASK_VALUE, _F32)
        for _ in range(MOBA_TOPK):
            top = jnp.max(g, axis=0, keepdims=True)
            is_top = (g == top) & (g > -jnp.inf)
            first = jnp.min(jnp.where(is_top, blk, float(nb)), axis=0, keepdims=True)
            pick = blk == first
            bias = jnp.where(pick, 0.0, bias)
            g = jnp.where(pick, -jnp.inf, g)
        bias_ref[h] = bias
        m_ref[h] = jnp.full((1, bq), MASK_VALUE, _F32)
        alpha_ref[1, h] = jnp.zeros((1, bq), _F32)
        acc_ref[h] = jnp.zeros(acc_ref.shape[1:], _F32)
        p_ref[1, h] = jnp.zeros(p_ref.shape[2:], p_ref.dtype)

    kidx = lax.broadcasted_iota(jnp.int32, (MOBA_BLOCK, bq), 0)
    qidx = lax.broadcasted_iota(jnp.int32, (MOBA_BLOCK, bq), 1)
    scores(n, 0, keep=kidx <= qidx)

    def two_steps(t, carry):
        step(2 * t, 0)
        step(2 * t + 1, 1)
        return carry
    lax.fori_loop(0, lax.shift_right_logical(n + 1, 1), two_steps, 0)

    last_block = jnp.maximum(n - 1, 0)

    @pl.when(lax.bitwise_and(n, 1) == 0)
    def _():
        step(n, 0)
        weigh_values(last_block, 0)

    @pl.when(lax.bitwise_and(n, 1) == 1)
    def _():
        weigh_values(last_block, 1)

    for h in range(heads):
        out_t = acc_ref[h, :HEAD_DIM, :] / acc_ref[h, HEAD_DIM:HEAD_DIM + 1, :]
        o_ref[:, head_cols[h]] = out_t.T.astype(o_ref.dtype)


def _moba_attention(q, k, vt, *, batch, seq, attn_width, heads_per_step, name):
    t = q.shape[0]
    nb = seq // MOBA_BLOCK
    gw = heads_per_step * HEAD_DIM
    return pl.pallas_call(
        functools.partial(_moba_kernel, heads=heads_per_step),
        grid=(batch, attn_width // gw, nb),
        in_specs=[
            pl.BlockSpec((MOBA_BLOCK, gw), lambda b, g, n: (b * nb + n, g)),
            pl.BlockSpec((heads_per_step, nb, MOBA_BLOCK, HEAD_DIM), lambda b, g, n: (g, b, 0, 0)),
            pl.BlockSpec((heads_per_step, nb, VT_ROWS, MOBA_BLOCK), lambda b, g, n: (g, b, 0, 0)),
        ],
        out_specs=pl.BlockSpec((MOBA_BLOCK, gw), lambda b, g, n: (b * nb + n, g)),
        out_shape=jax.ShapeDtypeStruct((t, attn_width), _BF16),
        scratch_shapes=[
            pltpu.VMEM((heads_per_step, nb, HEAD_DIM), _F32),
            pltpu.VMEM((heads_per_step, HEAD_DIM, MOBA_BLOCK), _BF16),
            pltpu.VMEM((heads_per_step, nb, MOBA_BLOCK), _F32),
            pltpu.VMEM((heads_per_step, 1, MOBA_BLOCK), _F32),
            pltpu.VMEM((2, heads_per_step, 1, MOBA_BLOCK), _F32),
            pltpu.VMEM((heads_per_step, VT_ROWS, MOBA_BLOCK), _F32),
            pltpu.VMEM((2, heads_per_step, MOBA_BLOCK, MOBA_BLOCK), _BF16),
            pltpu.VMEM((2, heads_per_step, MOBA_BLOCK, MOBA_BLOCK), _F32),
        ],
        compiler_params=_params("parallel", "parallel", "arbitrary"),
        name=name,
    )(q, k, vt)


def _ln_kernel(y_ref, g_ref, b_ref, o_ref, *maybe_stats_ref):
    y = y_ref[...]
    mu = jnp.mean(y, axis=-1, keepdims=True)
    yc = y - mu
    var = jnp.mean(yc * yc, axis=-1, keepdims=True)
    rstd = lax.rsqrt(var + LN_EPS)
    o_ref[...] = (yc * rstd * g_ref[...] + b_ref[...]).astype(o_ref.dtype)
    for stats_ref in maybe_stats_ref:
        stats_ref[:, :V7X_LANES] = jnp.broadcast_to(mu, (y.shape[0], V7X_LANES))
        stats_ref[:, V7X_LANES:] = jnp.broadcast_to(rstd, (y.shape[0], V7X_LANES))


def _layer_norm(y, gain, shift, *, out_dtype, with_stats, tm, name):
    m, n = y.shape
    tm = min(tm, m)
    out_shape = [jax.ShapeDtypeStruct((m, n), out_dtype)]
    out_specs = [pl.BlockSpec((tm, n), lambda i: (i, 0))]
    if with_stats:
        out_shape.append(jax.ShapeDtypeStruct((m, 2 * V7X_LANES), _F32))
        out_specs.append(pl.BlockSpec((tm, 2 * V7X_LANES), lambda i: (i, 0)))
    outs = pl.pallas_call(
        _ln_kernel,
        grid=(m // tm,),
        in_specs=[
            pl.BlockSpec((tm, n), lambda i: (i, 0)),
            pl.BlockSpec((1, n), lambda i: (0, 0)),
            pl.BlockSpec((1, n), lambda i: (0, 0)),
        ],
        out_specs=out_specs,
        out_shape=out_shape,
        compiler_params=_params("parallel"),
        name=name,
    )(y, gain, shift)
    return outs if with_stats else outs[0]


def _mm_resid_kernel(*refs, n_lhs, ln_resid):
    lhs_refs, w_ref, r_ref, o_ref = refs[:n_lhs], refs[n_lhs], refs[n_lhs + 1], refs[-1]
    acc, k0 = None, 0
    for a_ref in lhs_refs:
        part = _dot(a_ref[...], w_ref[k0:k0 + a_ref.shape[1], :])
        acc = part if acc is None else acc + part
        k0 += a_ref.shape[1]
    if not ln_resid:
        o_ref[...] = DEEPNORM_ALPHA * r_ref[...] + acc
        return
    stats_ref, g_ref, b_ref = refs[n_lhs + 2:n_lhs + 5]
    mu = stats_ref[:, :V7X_LANES]
    rstd = stats_ref[:, V7X_LANES:]
    for c in range(o_ref.shape[1] // V7X_LANES):
        cols = slice(c * V7X_LANES, (c + 1) * V7X_LANES)
        h = (r_ref[:, cols] - mu) * rstd * g_ref[:, cols] + b_ref[:, cols]
        o_ref[:, cols] = DEEPNORM_ALPHA * h + acc[:, cols]


def _column_tiles(w, tn):
    k, n = w.shape
    tn = min(tn, n)
    return w.astype(_BF16).reshape(k, n // tn, tn).transpose(1, 0, 2)


def _matmul_resid(lhs, w, resid, ln=None, *, tm, name):
    m = lhs[0][0].shape[0]
    n_tiles, k, tn = w.shape
    n = n_tiles * tn
    tm = min(tm, m)
    in_specs = [pl.BlockSpec((tm, width), functools.partial(lambda i, j, c: (i, c), c=col),
                             pipeline_mode=pl.Buffered(buffers))
                for _, width, col, buffers in lhs]
    in_specs += [pl.BlockSpec((None, k, tn), lambda i, j: (j, 0, 0)), pl.BlockSpec((tm, tn), lambda i, j: (i, j))]
    operands = [*(a for a, _, _, _ in lhs), w, resid]
    if ln is not None:
        in_specs += [pl.BlockSpec((tm, 2 * V7X_LANES), lambda i, j: (i, 0)),
                     pl.BlockSpec((1, tn), lambda i, j: (0, j)), pl.BlockSpec((1, tn), lambda i, j: (0, j))]
        operands += list(ln)
    return pl.pallas_call(
        functools.partial(_mm_resid_kernel, n_lhs=len(lhs), ln_resid=ln is not None),
        grid=(m // tm, n // tn),
        in_specs=in_specs,
        out_specs=pl.BlockSpec((tm, tn), lambda i, j: (i, j)),
        out_shape=jax.ShapeDtypeStruct((m, n), _F32),
        compiler_params=_params("parallel", "parallel"),
        name=name,
    )(*operands)


def _xattn_kernel(h_ref, wq_ref, k_ref, v_ref, o_ref, *, scale):
    q = _dot(h_ref[...], wq_ref[...]).astype(_BF16)
    s = _dot_nt(q, k_ref[...]) * scale
    m = jnp.max(s, axis=-1, keepdims=True)
    p = jnp.exp(s - m)
    l = jnp.sum(p, axis=-1, keepdims=True)
    o = _dot(p.astype(_BF16), v_ref[...]) / l
    o_ref[...] = o.astype(o_ref.dtype)


def _cross_attention(h, wq, kv, *, seq, mem_len, tm, name):
    t, d = h.shape
    hd = d // XATTN_HEADS
    tm = min(tm, seq)
    seq_tiles = seq // tm
    return pl.pallas_call(
        functools.partial(_xattn_kernel, scale=hd ** -0.5),
        grid=(t // tm, XATTN_HEADS),
        in_specs=[
            pl.BlockSpec((tm, d), lambda i, hh: (i, 0)),
            pl.BlockSpec((d, hd), lambda i, hh: (0, hh)),
            pl.BlockSpec((mem_len, hd), lambda i, hh: (i // seq_tiles, hh)),
            pl.BlockSpec((mem_len, hd), lambda i, hh: (i // seq_tiles, XATTN_HEADS + hh)),
        ],
        out_specs=pl.BlockSpec((tm, hd), lambda i, hh: (i, hh)),
        out_shape=jax.ShapeDtypeStruct((t, d), _BF16),
        compiler_params=_params("parallel", "parallel"),
        name=name,
    )(h, wq, kv, kv)


def _gate_up_kernel(x_ref, wg_ref, wu_ref, o_ref):
    x = x_ref[...]
    g = _dot(x, wg_ref[...].astype(x.dtype))
    u = _dot(x, wu_ref[...].astype(x.dtype))
    o_ref[...] = (g * jax.nn.sigmoid(g) * u).astype(o_ref.dtype)


def _gate_up(x, wg, wu, *, tm, tn, name):
    m, k = x.shape
    n = wg.shape[1]
    tm, tn = min(tm, m), min(tn, n)
    return pl.pallas_call(
        _gate_up_kernel,
        grid=(m // tm, n // tn),
        in_specs=[
            pl.BlockSpec((tm, k), lambda i, j: (i, 0), pipeline_mode=pl.Buffered(1)),
            pl.BlockSpec((k, tn), lambda i, j: (0, j)),
            pl.BlockSpec((k, tn), lambda i, j: (0, j)),
        ],
        out_specs=pl.BlockSpec((tm, tn), lambda i, j: (i, j)),
        out_shape=jax.ShapeDtypeStruct((m, n), _BF16),
        compiler_params=_params("parallel", "parallel"),
        name=name,
    )(x, wg, wu)


def _rope_tables(seq):
    half = HEAD_DIM // 2
    inv = ROPE_THETA ** (-jnp.arange(half, dtype=_F32) / half)
    ang = jnp.arange(seq, dtype=jnp.int32).astype(_F32)[:, None] * inv[None, :]
    cos, sin = jnp.cos(ang), jnp.sin(ang)
    return jnp.concatenate([cos, cos], axis=-1), jnp.concatenate([-sin, sin], axis=-1)


def _layer(h, mem_b, w_in, w_pool, pool_scale, w_out, ln1_g, ln1_b, w_xq, w_xkv, w_xo, ln2_g, ln2_b,
           w_gate, w_up, w_down, ln3_g, ln3_b, *, batch, seq):
    pool_width = w_pool.shape[0] * w_pool.shape[1]
    attn_width = (w_in.shape[1] - pool_width) // 3
    mem_len = mem_b.shape[0] // batch
    assert w_gate.shape[1] % FFN_TILE == 0

    w_in_b = w_in.astype(_BF16)
    cos, sin = _rope_tables(seq)
    row = lambda v: v.reshape(1, -1)
    ln1, ln2, ln3 = (row(ln1_g), row(ln1_b)), (row(ln2_g), row(ln2_b)), (row(ln3_g), row(ln3_b))

    hp, hb = _matmul_cast(h, w_in_b, col0=0, n=pool_width, tm=512, tn=pool_width, out_dtype=_BF16,
                          name="in_proj_pool")
    q = _matmul_rope(hb, w_in_b, cos, sin, col0=pool_width, n=attn_width, seq=seq, scale=HEAD_DIM ** -0.5 * LOG2_E,
                     blocked=False, tm=1024, tn=1024, name="in_proj_q_rope")
    k = _matmul_rope(hb, w_in_b, cos, sin, col0=pool_width + attn_width, n=attn_width, seq=seq, scale=1.0,
                     blocked=True, tm=1024, tn=1024, name="in_proj_k_rope")
    vt = _matmul_vt(hb, w_in_b, col0=pool_width + 2 * attn_width, n=attn_width, tm=1024, tn=1024,
                    name="in_proj_vt")
    pool_out = _pool_mixer(hp, w_pool.astype(_BF16), row(pool_scale), seq=seq, tm=512, name="pool_mixer")
    attn = _moba_attention(q, k, vt, batch=batch, seq=seq, attn_width=attn_width, heads_per_step=4,
                           name="moba_attention")
    y1 = _matmul_resid([(pool_out, pool_width, 0, 2), (attn, attn_width, 0, 2)], _column_tiles(w_out, 512), h,
                       tm=1024, name="mix_out")
    h1b, stats1 = _layer_norm(y1, *ln1, out_dtype=_BF16, with_stats=True, tm=256, name="ln1")

    kv = _matmul_wcast(mem_b, w_xkv, tm=512, tn=512, out_dtype=_BF16, name="xattn_kv")
    xo = _cross_attention(h1b, w_xq.astype(_BF16), kv, seq=seq, mem_len=mem_len, tm=1024, name="xattn_q_attend")
    y2 = _matmul_resid([(xo, xo.shape[1], 0, 2)], _column_tiles(w_xo, 512), y1, (stats1, *ln1), tm=1024,
                       name="xattn_out")
    h2b, stats2 = _layer_norm(y2, *ln2, out_dtype=_BF16, with_stats=True, tm=256, name="ln2")

    act = _gate_up(h2b, w_gate, w_up, tm=2048, tn=FFN_TILE, name="ffn_gate_up")
    y3 = _matmul_resid([(act, act.shape[1], 0, 1)], _column_tiles(w_down, 256), y2, (stats2, *ln2),
                       tm=1024, name="ffn_down")
    return _layer_norm(y3, *ln3, out_dtype=_F32, with_stats=False, tm=256, name="ln3")


def kernel(x, mem, w_mix_in, w_pool, pool_scale, w_mix_out, ln1_g, ln1_b, w_xq, w_xkv, w_xo, ln2_g, ln2_b,
           w_gate, w_up, w_down, ln3_g, ln3_b):
    batch, seq, d = x.shape
    assert w_mix_in.shape[0] == DEPTH
    h = x.reshape(batch * seq, d)
    mem_b = mem.reshape(-1, d).astype(_BF16)
    for l in range(DEPTH):
        h = _layer(h, mem_b, w_mix_in[l], w_pool[l], pool_scale[l], w_mix_out[l],
                   ln1_g[l], ln1_b[l], w_xq[l], w_xkv[l], w_xo[l], ln2_g[l], ln2_b[l],
                   w_gate[l], w_up[l], w_down[l], ln3_g[l], ln3_b[l], batch=batch, seq=seq)
    return h.reshape(batch, seq, d)
```
